```python
import math
import jax
import jax.numpy as jnp
from jax import lax
import numpy as np

D_MODEL = 1024
BATCH = 4
SEQ = 4096
DEPTH = 4
DEC_BATCH = 128
DEC_SEQ = 4
PAST_LEN = 8192
PAGE_SIZE = 128

N_MIXERS = 3
N_A = (DEPTH + 2) // 3
N_B = (DEPTH + 1) // 3
N_C = DEPTH // 3

DA_HEADS = 8
DA_KV_HEADS = 4
DA_GROUP = DA_HEADS // DA_KV_HEADS
DA_HEAD = 64
MLA_HEADS = 8
MLA_NOPE = 64
MLA_ROPE = 32
MLA_V = 64
MLA_Q_LORA = 256
MLA_KV_LORA = 256
MLA_SCALE = (MLA_NOPE + MLA_ROPE) ** -0.5
ROPE_THETA = 10000.0
HG_EXPAND = 128
HG_HEADS = D_MODEL // HG_EXPAND
HG_DK = HG_EXPAND
HG_DV = D_MODEL // HG_HEADS
HG_CHUNK = 64
D_FF = ((8 * D_MODEL // 3 + 255) // 256) * 256
Q_BLOCK = 128
EPS = 1e-6

kernel_name = 'hybrid_diffattn_mla_hgrn2_step'


def rms_norm(x, g):
    xf = x.astype(jnp.float32)
    y = xf * lax.rsqrt(jnp.mean(xf * xf, axis=-1, keepdims=True) + EPS)
    return (y * g.astype(jnp.float32)).astype(x.dtype)


def swiglu(h, w_gu, w_down):
    gu = h @ w_gu
    return (jax.nn.silu(gu[..., :D_FF]) * gu[..., D_FF:]) @ w_down


def rope_tables(pos):
    inv = ROPE_THETA ** (-jnp.arange(0, MLA_ROPE, 2, dtype=jnp.float32) / MLA_ROPE)
    ang = pos.astype(jnp.float32)[:, None] * inv[None, :]
    return jnp.cos(ang), jnp.sin(ang)


def apply_rope(x, cos, sin):
    x1, x2 = jnp.split(x.astype(jnp.float32), 2, axis=-1)
    return jnp.concatenate([x1 * cos - x2 * sin, x1 * sin + x2 * cos], axis=-1).astype(x.dtype)


def causal_block_softmax(s, blk, scale):
    qn, kn = s.shape[-2:]
    qpos = blk * Q_BLOCK + jnp.arange(qn)
    mask = jnp.arange(kn)[None, :] <= qpos[:, None]
    s = jnp.where(mask, s.astype(jnp.float32) * scale, -jnp.inf)
    return jax.nn.softmax(s, axis=-1)


def softmax_past_new(s_past, s_new, scale):
    t = s_new.shape[-1]
    causal = jnp.tril(jnp.ones((t, t), dtype=bool))
    s = jnp.concatenate([s_past.astype(jnp.float32),
                         jnp.where(causal, s_new.astype(jnp.float32), -jnp.inf)], axis=-1) * scale
    a = jax.nn.softmax(s, axis=-1)
    p = s_past.shape[-1]
    return a[..., :p], a[..., p:]


def diff_project(h, w_qkv, g_q, g_k):
    b, l, _ = h.shape
    nq = DA_HEADS * 2 * DA_HEAD
    nk = DA_KV_HEADS * 2 * DA_HEAD
    qkv = h @ w_qkv
    q = qkv[..., :nq].reshape(b, l, DA_KV_HEADS, DA_GROUP, 2, DA_HEAD)
    k = qkv[..., nq:nq + nk].reshape(b, l, DA_KV_HEADS, 2, DA_HEAD)
    v = qkv[..., nq + nk:].reshape(b, l, DA_KV_HEADS, 2 * DA_HEAD)
    return rms_norm(q, g_q), rms_norm(k, g_k), v


def diff_attend_prompt(q, k, v, lam):
    b, l = q.shape[:2]
    nb = l // Q_BLOCK
    qb = q.reshape(b, nb, Q_BLOCK, DA_KV_HEADS, DA_GROUP, 2, DA_HEAD).swapaxes(0, 1)

    def block(args):
        q_blk, blk = args
        s = jnp.einsum('bqhgmd,bkhmd->bhgmqk', q_blk, k)
        a = causal_block_softmax(s, blk, DA_HEAD ** -0.5)
        a = (a[..., 0, :, :] - lam * a[..., 1, :, :]).astype(v.dtype)
        return jnp.einsum('bhgqk,bkhe->bqhge', a, v)

    o = lax.map(block, (qb, jnp.arange(nb)))
    return o.swapaxes(0, 1).reshape(b, l, DA_HEADS, 2 * DA_HEAD)


def diff_attend_sample(q, k_new, v_new, k_past, v_past, lam):
    b, t = q.shape[:2]
    s_p = jnp.einsum('bthgmd,bphmd->bhgmtp', q, k_past)
    s_n = jnp.einsum('bthgmd,bshmd->bhgmts', q, k_new)
    a_p, a_n = softmax_past_new(s_p, s_n, DA_HEAD ** -0.5)
    a_p = (a_p[..., 0, :, :] - lam * a_p[..., 1, :, :]).astype(v_past.dtype)
    a_n = (a_n[..., 0, :, :] - lam * a_n[..., 1, :, :]).astype(v_new.dtype)
    o = jnp.einsum('bhgtp,bphe->bthge', a_p, v_past) + jnp.einsum('bhgts,bshe->bthge', a_n, v_new)
    return o.reshape(b, t, DA_HEADS, 2 * DA_HEAD)


def diff_output(o, lam_init, g_sub, w_o):
    b, l = o.shape[:2]
    o = rms_norm(o, g_sub) * (1.0 - lam_init)
    return o.reshape(b, l, DA_HEADS * 2 * DA_HEAD) @ w_o


def mla_project(h, pos, w_down, g_qa, g_kva, w_uq, g_qn, g_qr, g_kr):
    b, l, _ = h.shape
    d = h @ w_down
    cq = rms_norm(d[..., :MLA_Q_LORA], g_qa)
    ckv = rms_norm(d[..., MLA_Q_LORA:MLA_Q_LORA + MLA_KV_LORA], g_kva)
    kr = d[..., MLA_Q_LORA + MLA_KV_LORA:]
    q = (cq @ w_uq).reshape(b, l, MLA_HEADS, MLA_NOPE + MLA_ROPE)
    cos, sin = rope_tables(pos)
    q_nope = rms_norm(q[..., :MLA_NOPE], g_qn)
    q_rope = apply_rope(rms_norm(q[..., MLA_NOPE:], g_qr), cos[:, None, :], sin[:, None, :])
    k_rope = apply_rope(rms_norm(kr, g_kr), cos, sin)
    return q_nope, q_rope, ckv, k_rope


def mla_nope_keys(ckv, w_uk, g_kn):
    return rms_norm(jnp.einsum('blc,chd->blhd', ckv, w_uk), g_kn)


def mla_attend_prompt(qn, qr, ckv, kr, kn):
    b, l = qn.shape[:2]
    nb = l // Q_BLOCK
    qnb = qn.reshape(b, nb, Q_BLOCK, MLA_HEADS, MLA_NOPE).swapaxes(0, 1)
    qrb = qr.reshape(b, nb, Q_BLOCK, MLA_HEADS, MLA_ROPE).swapaxes(0, 1)

    def block(args):
        qn_b, qr_b, blk = args
        s = jnp.einsum('bqhd,bkhd->bhqk', qn_b, kn) + jnp.einsum('bqhr,bkr->bhqk', qr_b, kr)
        a = causal_block_softmax(s, blk, MLA_SCALE).astype(ckv.dtype)
        return jnp.einsum('bhqk,bkc->bqhc', a, ckv)

    o = lax.map(block, (qnb, qrb, jnp.arange(nb)))
    return o.swapaxes(0, 1).reshape(b, l, MLA_HEADS, MLA_KV_LORA)


def mla_attend_sample(qn, qr, c_new, r_new, kn_new, c_past, r_past, kn_past):
    s_p = jnp.einsum('bthd,bphd->bhtp', qn, kn_past) + jnp.einsum('bthr,bpr->bhtp', qr, r_past)
    s_n = jnp.einsum('bthd,bshd->bhts', qn, kn_new) + jnp.einsum('bthr,bsr->bhts', qr, r_new)
    a_p, a_n = softmax_past_new(s_p, s_n, MLA_SCALE)
    return (jnp.einsum('bhtp,bpc->bthc', a_p.astype(c_past.dtype), c_past)
            + jnp.einsum('bhts,bsc->bthc', a_n.astype(c_new.dtype), c_new))


def mla_output(o_lat, w_uv, w_o):
    b, l = o_lat.shape[:2]
    o = jnp.einsum('blhc,chv->blhv', o_lat, w_uv)
    return o.reshape(b, l, MLA_HEADS * MLA_V) @ w_o


def hgrn2_project(h, w_in, lb):
    b, l, _ = h.shape
    nk = HG_HEADS * HG_DK
    nv = HG_HEADS * HG_DV
    z = h @ w_in
    q = jax.nn.silu(z[..., :nk].astype(jnp.float32)).reshape(b, l, HG_HEADS, HG_DK)
    f = lb + (1.0 - lb) * jax.nn.sigmoid(z[..., nk:2 * nk].astype(jnp.float32))
    k = (1.0 - f).reshape(b, l, HG_HEADS, HG_DK)
    logf = jnp.log(f).reshape(b, l, HG_HEADS, HG_DK)
    v = z[..., 2 * nk:2 * nk + nv].astype(jnp.float32).reshape(b, l, HG_HEADS, HG_DV)
    g = z[..., 2 * nk + nv:]
    return q, k, logf, v, g


def hgrn2_chunked(q, k, logf, v, s0):
    b, l, h, _ = q.shape
    c = HG_CHUNK if l % HG_CHUNK == 0 else l
    n = l // c

    def to_chunks(t):
        return t.reshape(b, n, c, h, t.shape[-1]).transpose(1, 0, 3, 2, 4)

    tri = jnp.tril(jnp.ones((c, c), dtype=bool))

    def step(s, inp):
        qc, kc, lc, vc = inp
        cum = jnp.cumsum(lc, axis=2)
        diff = cum[:, :, :, None, :] - cum[:, :, None, :, :]
        decay = jnp.exp(jnp.where(tri[:, :, None], diff, -jnp.inf))
        a = jnp.einsum('bhtd,bhsd,bhtsd->bhts', qc, kc, decay)
        o = jnp.einsum('bhts,bhse->bhte', a, vc) + jnp.einsum('bhtd,bhde->bhte', qc * jnp.exp(cum), s)
        last = cum[:, :, -1:, :]
        s_new = (jnp.exp(last[:, :, 0, :])[..., None] * s
                 + jnp.einsum('bhsd,bhse->bhde', kc * jnp.exp(last - cum), vc))
        return s_new, o

    s_fin, o = lax.scan(step, s0, (to_chunks(q), to_chunks(k), to_chunks(logf), to_chunks(v)))
    return o.transpose(1, 0, 3, 2, 4).reshape(b, l, h, HG_DV), s_fin


def hgrn2_output(o, g, g_out, w_o):
    b, l = o.shape[:2]
    o = rms_norm(o, g_out).reshape(b, l, HG_HEADS * HG_DV).astype(g.dtype)
    return (o * jax.nn.silu(g)) @ w_o


def setup_inputs(seed: int = 0) -> dict:
    key = jax.random.key(seed)
    keys = iter(jax.random.split(key, 64))

    def nrm(shape, scale=1.0):
        return jax.random.normal(next(keys), shape, jnp.float32) * scale

    def gain(shape):
        return 1.0 + nrm(shape, 0.05)

    n_pages = PAST_LEN // PAGE_SIZE
    n_used = DEC_BATCH * n_pages
    n_pool = n_used + n_used // 4
    page_table = jax.random.permutation(next(keys), n_pool)[:n_used].reshape(DEC_BATCH, n_pages).astype(jnp.int32)
    da_cols = DA_HEADS * 2 * DA_HEAD + 2 * DA_KV_HEADS * 2 * DA_HEAD
    mla_down_cols = MLA_Q_LORA + MLA_KV_LORA + MLA_ROPE
    hg_cols = 2 * HG_HEADS * HG_DK + 2 * HG_HEADS * HG_DV
    return {
        'x_prompt': nrm((BATCH, SEQ, D_MODEL)),
        'x_sample': nrm((DEC_BATCH, DEC_SEQ, D_MODEL)),
        'cache_da_k': nrm((N_A, n_pool, PAGE_SIZE, DA_KV_HEADS, 2, DA_HEAD)),
        'cache_da_v': nrm((N_A, n_pool, PAGE_SIZE, DA_KV_HEADS, 2 * DA_HEAD)),
        'cache_mla_ckv': nrm((N_B, n_pool, PAGE_SIZE, MLA_KV_LORA)),
        'cache_mla_krope': nrm((N_B, n_pool, PAGE_SIZE, MLA_ROPE)),
        'state_hgrn': nrm((N_C, DEC_BATCH, HG_HEADS, HG_DK, HG_DV), 0.5),
        'page_table': page_table,
        'norm_mix': gain((DEPTH, D_MODEL)),
        'norm_ffn': gain((DEPTH, D_MODEL)),
        'ffn_w_gu': nrm((DEPTH, D_MODEL, 2 * D_FF), D_MODEL ** -0.5),
        'ffn_w_down': nrm((DEPTH, D_FF, D_MODEL), D_FF ** -0.5),
        'da_w_qkv': nrm((N_A, D_MODEL, da_cols), D_MODEL ** -0.5),
        'da_q_gain': gain((N_A, DA_HEAD)),
        'da_k_gain': gain((N_A, DA_HEAD)),
        'da_lambda': nrm((N_A, 4, DA_HEAD), 0.1),
        'da_sub_gain': gain((N_A, 2 * DA_HEAD)),
        'da_w_o': nrm((N_A, DA_HEADS * 2 * DA_HEAD, D_MODEL), (DA_HEADS * 2 * DA_HEAD) ** -0.5),
        'mla_w_down': nrm((N_B, D_MODEL, mla_down_cols), D_MODEL ** -0.5),
        'mla_qa_gain': gain((N_B, MLA_Q_LORA)),
        'mla_kva_gain': gain((N_B, MLA_KV_LORA)),
        'mla_w_uq': nrm((N_B, MLA_Q_LORA, MLA_HEADS * (MLA_NOPE + MLA_ROPE)), MLA_Q_LORA ** -0.5),
        'mla_w_uk': nrm((N_B, MLA_KV_LORA, MLA_HEADS, MLA_NOPE), MLA_KV_LORA ** -0.5),
        'mla_w_uv': nrm((N_B, MLA_KV_LORA, MLA_HEADS, MLA_V), MLA_KV_LORA ** -0.5),
        'mla_qn_gain': gain((N_B, MLA_NOPE)),
        'mla_qr_gain': gain((N_B, MLA_ROPE)),
        'mla_kn_gain': gain((N_B, MLA_NOPE)),
        'mla_kr_gain': gain((N_B, MLA_ROPE)),
        'mla_w_o': nrm((N_B, MLA_HEADS * MLA_V, D_MODEL), (MLA_HEADS * MLA_V) ** -0.5),
        'hg_w_in': nrm((N_C, D_MODEL, hg_cols), D_MODEL ** -0.5),
        'hg_lb_logits': nrm((DEPTH, HG_HEADS * HG_DK), 0.5),
        'hg_out_gain': gain((N_C, HG_DV)),
        'hg_w_o': nrm((N_C, HG_HEADS * HG_DV, D_MODEL), (HG_HEADS * HG_DV) ** -0.5),
    }


def reference(x_prompt, x_sample, cache_da_k, cache_da_v, cache_mla_ckv, cache_mla_krope, state_hgrn, page_table,
              norm_mix, norm_ffn, ffn_w_gu, ffn_w_down,
              da_w_qkv, da_q_gain, da_k_gain, da_lambda, da_sub_gain, da_w_o,
              mla_w_down, mla_qa_gain, mla_kva_gain, mla_w_uq, mla_w_uk, mla_w_uv,
              mla_qn_gain, mla_qr_gain, mla_kn_gain, mla_kr_gain, mla_w_o,
              hg_w_in, hg_lb_logits, hg_out_gain, hg_w_o):
    n_batch, seq = x_prompt.shape[:2]
    n_seq, n_new = x_sample.shape[:2]
    past = page_table.shape[1] * PAGE_SIZE
    pos_p = jnp.arange(seq)
    pos_s = past + jnp.arange(n_new)
    lb_soft = jax.nn.softmax(hg_lb_logits.astype(jnp.float32), axis=0)
    lower_bounds = jnp.cumsum(lb_soft, axis=0) - lb_soft[0]

    da_kp, da_vp, da_ks, da_vs = [], [], [], []
    mla_cp, mla_rp, mla_cs, mla_rs = [], [], [], []
    hg_p, hg_s = [], []
    xp, xs = x_prompt, x_sample
    for i in range(DEPTH):
        kind, j = i % N_MIXERS, i // N_MIXERS
        hp = rms_norm(xp, norm_mix[i])
        hs = rms_norm(xs, norm_mix[i])
        if kind == 0:
            lam_init = 0.8 - 0.6 * math.exp(-0.3 * i)
            lv = da_lambda[j].astype(jnp.float32)
            lam = jnp.exp(jnp.sum(lv[0] * lv[1])) - jnp.exp(jnp.sum(lv[2] * lv[3])) + lam_init
            qp, kp, vp = diff_project(hp, da_w_qkv[j], da_q_gain[j], da_k_gain[j])
            mp = diff_output(diff_attend_prompt(qp, kp, vp, lam), lam_init, da_sub_gain[j], da_w_o[j])
            qs, ks, vs = diff_project(hs, da_w_qkv[j], da_q_gain[j], da_k_gain[j])
            k_past = cache_da_k[j, page_table].reshape(n_seq, past, DA_KV_HEADS, 2, DA_HEAD)
            v_past = cache_da_v[j, page_table].reshape(n_seq, past, DA_KV_HEADS, 2 * DA_HEAD)
            ms = diff_output(diff_attend_sample(qs, ks, vs, k_past, v_past, lam), lam_init, da_sub_gain[j], da_w_o[j])
            da_kp.append(kp)
            da_vp.append(vp)
            da_ks.append(ks)
            da_vs.append(vs)
        elif kind == 1:
            qn_p, qr_p, c_p, r_p = mla_project(hp, pos_p, mla_w_down[j], mla_qa_gain[j], mla_kva_gain[j],
                                               mla_w_uq[j], mla_qn_gain[j], mla_qr_gain[j], mla_kr_gain[j])
            kn_p = mla_nope_keys(c_p, mla_w_uk[j], mla_kn_gain[j])
            mp = mla_output(mla_attend_prompt(qn_p, qr_p, c_p, r_p, kn_p), mla_w_uv[j], mla_w_o[j])
            qn_s, qr_s, c_s, r_s = mla_project(hs, pos_s, mla_w_down[j], mla_qa_gain[j], mla_kva_gain[j],
                                               mla_w_uq[j], mla_qn_gain[j], mla_qr_gain[j], mla_kr_gain[j])
            c_past = cache_mla_ckv[j, page_table].reshape(n_seq, past, MLA_KV_LORA)
            r_past = cache_mla_krope[j, page_table].reshape(n_seq, past, MLA_ROPE)
            kn_past = mla_nope_keys(c_past, mla_w_uk[j], mla_kn_gain[j])
            kn_s = mla_nope_keys(c_s, mla_w_uk[j], mla_kn_gain[j])
            ms = mla_output(mla_attend_sample(qn_s, qr_s, c_s, r_s, kn_s, c_past, r_past, kn_past),
                            mla_w_uv[j], mla_w_o[j])
            mla_cp.append(c_p)
            mla_rp.append(r_p)
            mla_cs.append(c_s)
            mla_rs.append(r_s)
        else:
            lb = lower_bounds[i]
            q, k, lf, v, g = hgrn2_project(hp, hg_w_in[j], lb)
            s0 = jnp.zeros((n_batch, HG_HEADS, HG_DK, HG_DV), jnp.float32)
            o, s_p = hgrn2_chunked(q, k, lf, v, s0)
            mp = hgrn2_output(o, g, hg_out_gain[j], hg_w_o[j])
            q, k, lf, v, g = hgrn2_project(hs, hg_w_in[j], lb)
            o, s_s = hgrn2_chunked(q, k, lf, v, state_hgrn[j].astype(jnp.float32))
            ms = hgrn2_output(o, g, hg_out_gain[j], hg_w_o[j])
            hg_p.append(s_p.astype(state_hgrn.dtype))
            hg_s.append(s_s.astype(state_hgrn.dtype))
        xp = xp + mp
        xs = xs + ms
        xp = xp + swiglu(rms_norm(xp, norm_ffn[i]), ffn_w_gu[i], ffn_w_down[i])
        xs = xs + swiglu(rms_norm(xs, norm_ffn[i]), ffn_w_gu[i], ffn_w_down[i])

    return (xp, xs,
            jnp.stack(da_kp), jnp.stack(da_vp), jnp.stack(da_ks), jnp.stack(da_vs),
            jnp.stack(mla_cp), jnp.stack(mla_rp), jnp.stack(mla_cs), jnp.stack(mla_rs),
            jnp.stack(hg_p), jnp.stack(hg_s))
```

```python
import functools
import math

import jax
import jax.numpy as jnp
from jax import lax
from jax.experimental import pallas as pl
from jax.experimental.pallas import tpu as pltpu

F32 = jnp.float32
BF16 = jnp.bfloat16

D_MODEL = 1024
PAGE_SIZE = 128
N_MIXERS = 3
DA_HEADS = 8
DA_KV_HEADS = 4
DA_GROUP = DA_HEADS // DA_KV_HEADS
DA_HEAD = 64
MLA_HEADS = 8
MLA_NOPE = 64
MLA_ROPE = 32
MLA_V = 64
MLA_Q_LORA = 256
MLA_KV_LORA = 256
MLA_SCALE = (MLA_NOPE + MLA_ROPE) ** -0.5
ROPE_THETA = 10000.0
HG_HEADS = 8
HG_DK = 128
HG_DV = 128
D_FF = 2816
EPS = 1e-6

NEG = -1e30
LANES = 128
MXU_DIM = 256
VMEM_LIMIT = 56 * 1024 * 1024
PAGES_PER_STEP = 8
HG_CHUNK = 128
HG_SUB = 16
HG_SEQ_BLOCK = 8


def _pick_tile(n, candidates=(512, 256, 128, 64, 32, 16, 8)):
    for c in candidates:
        if n % c == 0:
            return c
    raise ValueError(f"no tile divides {n}")


def _const_spec(shape):
    nd = len(shape)
    return pl.BlockSpec(shape, lambda *_: (0,) * nd, pipeline_mode=pl.Buffered(1))


def _params(sem):
    return pltpu.CompilerParams(dimension_semantics=sem, vmem_limit_bytes=VMEM_LIMIT)


def _rms(x, g):
    ms = jnp.mean(x * x, axis=-1, keepdims=True)
    return x * lax.rsqrt(ms + EPS) * g


def _dot(a, b):
    return jnp.dot(a, b, preferred_element_type=F32)


def _dot_nt(a, b):
    return lax.dot_general(a, b, (((1,), (1,)), ((), ())), preferred_element_type=F32)


def _group_ones(width, groups):
    idx = jnp.arange(width)
    gid = jnp.full((width,), -1, jnp.int32)
    for n, (a, b) in enumerate(groups):
        gid = jnp.where((idx >= a) & (idx < b), n, gid)
    m = (gid[:, None] == gid[None, :]) & (gid[:, None] >= 0)
    return m.astype(BF16)


def _lambda_value(lv, lam_init):
    a = jnp.sum(lv[0:1] * lv[1:2], axis=-1, keepdims=True)
    b = jnp.sum(lv[2:3] * lv[3:4], axis=-1, keepdims=True)
    return jnp.exp(a) - jnp.exp(b) + lam_init


def _col_replicated(row):
    return jnp.broadcast_to(row, (LANES, LANES)).T


def _da_proj_kernel(x_ref, gmix_ref, w_ref, gq_ref, gk_ref, ones_ref,
                    q_ref, kf_ref, vf_ref, kb_ref, vb_ref):
    h = _rms(x_ref[...], gmix_ref[...]).astype(BF16)
    y = _dot(h, w_ref[...])
    ones = ones_ref[...]

    def group_norm(yc, g):
        ss = _dot((yc * yc).astype(BF16), ones)
        return yc * lax.rsqrt(ss * (1.0 / DA_HEAD) + EPS) * g

    nq = DA_HEADS * 2 * DA_HEAD
    nk = DA_KV_HEADS * 2 * DA_HEAD
    for c in range(nq // MXU_DIM):
        sl = slice(c * MXU_DIM, (c + 1) * MXU_DIM)
        q_ref[:, sl] = group_norm(y[:, sl], gq_ref[:, sl]).astype(BF16)
    for c in range(nk // MXU_DIM):
        sl = slice(c * MXU_DIM, (c + 1) * MXU_DIM)
        kn = group_norm(y[:, nq + c * MXU_DIM:nq + (c + 1) * MXU_DIM], gk_ref[:, sl])
        kf_ref[:, sl] = kn
        kb_ref[:, sl] = kn.astype(BF16)
    v = y[:, nq + nk:]
    vf_ref[...] = v
    vb_ref[...] = v.astype(BF16)


def _da_project(x, gmix, w_qkv, g_q, g_k):
    t = x.shape[0]
    tm = _pick_tile(t)
    nq = DA_HEADS * 2 * DA_HEAD
    nk = DA_KV_HEADS * 2 * DA_HEAD
    gq_row = (jnp.tile(g_q.astype(F32), nq // DA_HEAD) * (DA_HEAD ** -0.5)).reshape(1, nq)
    gk_row = jnp.tile(g_k.astype(F32), nk // DA_HEAD).reshape(1, nk)
    ones = _group_ones(MXU_DIM, [(i * DA_HEAD, (i + 1) * DA_HEAD) for i in range(MXU_DIM // DA_HEAD)])
    row = lambda n: pl.BlockSpec((tm, n), lambda i: (i, 0))
    return pl.pallas_call(
        _da_proj_kernel,
        grid=(t // tm,),
        in_specs=[row(D_MODEL), _const_spec((1, D_MODEL)), _const_spec(w_qkv.shape),
                  _const_spec((1, nq)), _const_spec((1, nk)), _const_spec(ones.shape)],
        out_specs=[row(nq), row(nk), row(nk), row(nk), row(nk)],
        out_shape=[jax.ShapeDtypeStruct((t, nq), BF16), jax.ShapeDtypeStruct((t, nk), F32),
                   jax.ShapeDtypeStruct((t, nk), F32), jax.ShapeDtypeStruct((t, nk), BF16),
                   jax.ShapeDtypeStruct((t, nk), BF16)],
        compiler_params=_params(("parallel",)),
        name="da_proj",
    )(x, gmix.reshape(1, D_MODEL).astype(F32), w_qkv.astype(BF16), gq_row, gk_row, ones)


def _da_attn_kernel(lam_ref, q_ref, k_ref, v_ref, gsub_ref, o_ref,
                    qs_sc, m_sc, l_sc, acc_sc, *, tq, lam_init):
    i = pl.program_id(2)
    rows = 2 * DA_GROUP * tq
    q = q_ref[...]
    lane = lax.broadcasted_iota(jnp.int32, (tq, LANES), 1)
    first_map = lane < DA_HEAD
    zero = jnp.zeros((tq, LANES), BF16)
    for g in range(DA_GROUP):
        qg = q[:, g * LANES:(g + 1) * LANES]
        qs_sc[(2 * g) * tq:(2 * g + 1) * tq, :] = jnp.where(first_map, qg, zero)
        qs_sc[(2 * g + 1) * tq:(2 * g + 2) * tq, :] = jnp.where(first_map, zero, qg)
    m_sc[...] = jnp.full((rows, LANES), NEG, F32)
    l_sc[...] = jnp.zeros((rows, LANES), F32)
    acc_sc[...] = jnp.zeros((rows, LANES), F32)

    def step(j, masked):
        off = pl.multiple_of(j * tq, tq)
        kj = k_ref[pl.ds(off, tq), :]
        vj = v_ref[pl.ds(off, tq), :]
        s = _dot_nt(qs_sc[...], kj)
        if masked:
            r = lax.broadcasted_iota(jnp.int32, (rows, tq), 0) & (tq - 1)
            c = lax.broadcasted_iota(jnp.int32, (rows, tq), 1)
            s = jnp.where(c <= r, s, NEG)
        m_prev = m_sc[...]
        m_new = jnp.maximum(m_prev, jnp.max(s, axis=-1, keepdims=True))
        alpha = jnp.exp(m_prev - m_new)
        p = jnp.exp(s - jnp.tile(m_new, (1, tq // LANES)))
        l_sc[...] = alpha * l_sc[...] + jnp.sum(p, axis=-1, keepdims=True)
        m_sc[...] = m_new
        acc_sc[...] = alpha * acc_sc[...] + _dot(p.astype(BF16), vj)

    def body(j, carry):
        step(j, False)
        return carry

    lax.fori_loop(0, i, body, 0)
    step(i, True)

    lam = _lambda_value(lam_ref[...], lam_init)
    a = acc_sc[...] / l_sc[...]
    for g in range(DA_GROUP):
        o = a[(2 * g) * tq:(2 * g + 1) * tq] - lam * a[(2 * g + 1) * tq:(2 * g + 2) * tq]
        on = _rms(o, gsub_ref[...]) * (1.0 - lam_init)
        o_ref[:, g * LANES:(g + 1) * LANES] = on.astype(BF16)


def _da_attend_prompt(q, kb, vb, lam_p, g_sub, lam_init, batch, seq):
    tq = _pick_tile(seq, (512, 256, 128))
    nq = seq // tq
    rows = 2 * DA_GROUP * tq
    kern = functools.partial(_da_attn_kernel, tq=tq, lam_init=lam_init)
    return pl.pallas_call(
        kern,
        grid=(batch, DA_KV_HEADS, nq),
        in_specs=[_const_spec(lam_p.shape),
                  pl.BlockSpec((tq, 2 * LANES), lambda b, h, i: (b * nq + i, h)),
                  pl.BlockSpec((seq, LANES), lambda b, h, i: (b, h)),
                  pl.BlockSpec((seq, LANES), lambda b, h, i: (b, h)),
                  _const_spec((1, LANES))],
        out_specs=pl.BlockSpec((tq, 2 * LANES), lambda b, h, i: (b * nq + i, h)),
        out_shape=jax.ShapeDtypeStruct((batch * seq, DA_HEADS * 2 * DA_HEAD), BF16),
        scratch_shapes=[pltpu.VMEM((rows, LANES), BF16), pltpu.VMEM((rows, LANES), F32),
                        pltpu.VMEM((rows, LANES), F32), pltpu.VMEM((rows, LANES), F32)],
        compiler_params=_params(("parallel", "parallel", "arbitrary")),
        name="da_attn_prompt",
    )(lam_p, q, kb, vb, g_sub.reshape(1, LANES).astype(F32))


def _softmax_update_cols(s, vb, m_sc, l_sc, acc_sc):
    m_prev = m_sc[...]
    m_new = jnp.maximum(m_prev, jnp.max(s, axis=0, keepdims=True))
    alpha = jnp.exp(m_prev - m_new)
    p = jnp.exp(s - m_new)
    l_sc[...] = alpha * l_sc[...] + jnp.sum(p, axis=0, keepdims=True)
    m_sc[...] = m_new
    p_t = p.T.astype(BF16)
    a_col = _col_replicated(alpha)
    width = acc_sc.shape[1]
    acc_sc[...] = acc_sc[...] * jnp.tile(a_col, (1, width // LANES)) + _dot(p_t, vb)


def _da_sample_kernel(pt_ref, lam_ref, qbd_ref, kn_ref, vn_ref, gsub_ref, *rest, n_pp, lam_init):
    k_pages = rest[:n_pp]
    v_pages = rest[n_pp:2 * n_pp]
    o_ref = rest[2 * n_pp]
    kbuf, vbuf, m_sc, l_sc, acc_sc = rest[2 * n_pp + 1:]
    del pt_ref
    c = pl.program_id(1)

    @pl.when(c == 0)
    def _():
        m_sc[...] = jnp.full(m_sc.shape, NEG, F32)
        l_sc[...] = jnp.zeros(l_sc.shape, F32)
        acc_sc[...] = jnp.zeros(acc_sc.shape, F32)

    qbd = qbd_ref[...]
    for r in range(n_pp):
        kbuf[r * PAGE_SIZE:(r + 1) * PAGE_SIZE, :] = k_pages[r][...].astype(BF16)
        vbuf[r * PAGE_SIZE:(r + 1) * PAGE_SIZE, :] = v_pages[r][...].astype(BF16)
    _softmax_update_cols(_dot(kbuf[...], qbd), vbuf[...], m_sc, l_sc, acc_sc)

    @pl.when(c == pl.num_programs(1) - 1)
    def _():
        row = lax.broadcasted_iota(jnp.int32, (LANES, LANES), 0)
        col = lax.broadcasted_iota(jnp.int32, (LANES, LANES), 1)
        t_col = (col >> 1) & 3
        s = jnp.where(row <= t_col, _dot(kn_ref[...], qbd), NEG)
        _softmax_update_cols(s, vn_ref[...], m_sc, l_sc, acc_sc)

        lam = _lambda_value(lam_ref[...], lam_init)
        inv = 1.0 / _col_replicated(l_sc[...])
        acc = acc_sc[...]
        half = 4 * DA_GROUP
        for h in range(DA_KV_HEADS):
            blk = acc[h * 2 * half:(h + 1) * 2 * half, h * LANES:(h + 1) * LANES]
            blk = blk * inv[h * 2 * half:(h + 1) * 2 * half, :]
            o = blk[:half] - lam * blk[half:]
            o_ref[h] = _rms(o, gsub_ref[...]) * (1.0 - lam_init)


def _da_attend_sample(q_s, kb_s, vb_s, cache_k, cache_v, page_table, lam_p, g_sub, lam_init, layer):
    n_seq, n_new = q_s.shape[0] // 4, 4
    n_pages = page_table.shape[1]
    n_pp = min(PAGES_PER_STEP, n_pages)
    n_chunks = n_pages // n_pp
    kvw = DA_KV_HEADS * 2 * DA_HEAD
    q6 = q_s.reshape(n_seq, n_new, DA_KV_HEADS, DA_GROUP, 2, DA_HEAD)
    eye_h = jnp.eye(DA_KV_HEADS, dtype=BF16)
    eye_m = jnp.eye(2, dtype=BF16)
    qbd = jnp.einsum('btkgmd,kK,mM->bkmdKMtg', q6, eye_h, eye_m).reshape(n_seq, kvw, 64)
    qbd = jnp.pad(qbd, ((0, 0), (0, 0), (0, LANES - 64)))
    pad_rows = lambda a: jnp.pad(a.reshape(n_seq, n_new, kvw), ((0, 0), (0, LANES - n_new), (0, 0)))
    k_new, v_new = pad_rows(kb_s), pad_rows(vb_s)
    n_pool = cache_k.shape[1]
    ck = cache_k.reshape(cache_k.shape[0], n_pool, PAGE_SIZE, kvw)
    cv = cache_v.reshape(cache_v.shape[0], n_pool, PAGE_SIZE, kvw)

    def page_spec(r):
        return pl.BlockSpec((None, None, PAGE_SIZE, kvw),
                            lambda b, c, pt: (layer, pt[b * n_pages + c * n_pp + r], 0, 0))

    per_seq = lambda shape: pl.BlockSpec((None,) + shape, lambda b, c, pt: (b,) + (0,) * len(shape))
    const = lambda shape: pl.BlockSpec(shape, lambda b, c, pt: (0,) * len(shape))
    kern = functools.partial(_da_sample_kernel, n_pp=n_pp, lam_init=lam_init)
    out = pl.pallas_call(
        kern,
        grid_spec=pltpu.PrefetchScalarGridSpec(
            num_scalar_prefetch=1,
            grid=(n_seq, n_chunks),
            in_specs=[const(lam_p.shape), per_seq((kvw, LANES)), per_seq((LANES, kvw)),
                      per_seq((LANES, kvw)), const((1, LANES))]
                     + [page_spec(r) for r in range(n_pp)] + [page_spec(r) for r in range(n_pp)],
            out_specs=per_seq((DA_KV_HEADS, 4 * DA_GROUP, LANES)),
            scratch_shapes=[pltpu.VMEM((n_pp * PAGE_SIZE, kvw), BF16),
                            pltpu.VMEM((n_pp * PAGE_SIZE, kvw), BF16),
                            pltpu.VMEM((1, LANES), F32), pltpu.VMEM((1, LANES), F32),
                            pltpu.VMEM((LANES, kvw), F32)]),
        out_shape=jax.ShapeDtypeStruct((n_seq, DA_KV_HEADS, 4 * DA_GROUP, LANES), F32),
        compiler_params=_params(("parallel", "arbitrary")),
        name="da_attn_sample",
    )(page_table.reshape(-1), lam_p, qbd, k_new, v_new, g_sub.reshape(1, LANES).astype(F32),
      *([ck] * n_pp), *([cv] * n_pp))
    out = out.reshape(n_seq, DA_KV_HEADS, n_new, DA_GROUP, LANES).transpose(0, 2, 1, 3, 4)
    return out.reshape(n_seq * n_new, DA_HEADS * 2 * DA_HEAD).astype(BF16)


def _post_kernel(x_ref, a_ref, wo_ref, gffn_ref, wgu_ref, wd_ref, o_ref, *, n_chunks):
    x1 = x_ref[...] + _dot(a_ref[...], wo_ref[...])
    h = _rms(x1, gffn_ref[...]).astype(BF16)
    ck = D_FF // n_chunks
    acc = x1
    for c in range(n_chunks):
        g = _dot(h, wgu_ref[:, c * ck:(c + 1) * ck])
        u = _dot(h, wgu_ref[:, D_FF + c * ck:D_FF + (c + 1) * ck])
        act = (g * jax.nn.sigmoid(g) * u).astype(BF16)
        acc = acc + _dot(act, wd_ref[c * ck:(c + 1) * ck, :])
    o_ref[...] = acc


def _post(x, a, w_o, g_ffn, w_gu, w_down):
    t = x.shape[0]
    tm = _pick_tile(t)
    ka = a.shape[1]
    kern = functools.partial(_post_kernel, n_chunks=2)
    return pl.pallas_call(
        kern,
        grid=(t // tm,),
        in_specs=[pl.BlockSpec((tm, D_MODEL), lambda i: (i, 0)),
                  pl.BlockSpec((tm, ka), lambda i: (i, 0)),
                  _const_spec(w_o.shape), _const_spec((1, D_MODEL)),
                  _const_spec(w_gu.shape), _const_spec(w_down.shape)],
        out_specs=pl.BlockSpec((tm, D_MODEL), lambda i: (i, 0)),
        out_shape=jax.ShapeDtypeStruct((t, D_MODEL), F32),
        compiler_params=_params(("parallel",)),
        name="post_ffn",
    )(x, a, w_o.astype(BF16), g_ffn.reshape(1, D_MODEL).astype(F32), w_gu.astype(BF16),
      w_down.astype(BF16))


MLA_BLK = 128
ROPE_HALF = MLA_ROPE // 2


def _mla_proj_kernel(x_ref, gmix_ref, cos_ref, sin_ref, wd_ref, gqa_ref, gkva_ref, gkra_ref, gkrb_ref,
                     wuqa_ref, wuqb_ref, ga_ref, gb_ref, invn_ref, onesq_ref,
                     wuk_ref, gkn_ref, onesk_ref, wuv_ref,
                     qcat_ref, kcat_ref, v_ref, ckvf_ref, kr_ref):
    h = _rms(x_ref[...], gmix_ref[...]).astype(BF16)
    d = _dot(h, wd_ref[...])
    cq = _rms(d[:, :MLA_Q_LORA], gqa_ref[...]).astype(BF16)
    ckv = _rms(d[:, MLA_Q_LORA:MLA_Q_LORA + MLA_KV_LORA], gkva_ref[...])
    ckvf_ref[...] = ckv
    ckv_b = ckv.astype(BF16)
    cos = cos_ref[...]
    sin = sin_ref[...]
    base = MLA_Q_LORA + MLA_KV_LORA
    ka = d[:, base:base + MLA_BLK]
    kb = d[:, base + MLA_BLK:base + 2 * MLA_BLK]
    rk = lax.rsqrt(jnp.sum(ka * ka, axis=-1, keepdims=True) * (1.0 / MLA_ROPE) + EPS)
    kr = rk * (ka * gkra_ref[...] * cos + kb * gkrb_ref[...] * sin)
    kr_ref[...] = kr
    ya = _dot(cq, wuqa_ref[...])
    yb = _dot(cq, wuqb_ref[...])
    reps = MXU_DIM // MLA_BLK
    cos2 = jnp.tile(cos, (1, reps))
    sin2 = jnp.tile(sin, (1, reps))
    kr2 = jnp.tile(kr, (1, reps))
    onesq = onesq_ref[...]
    onesk = onesk_ref[...]
    kk = _dot(ckv_b, wuk_ref[...])
    for c in range(MLA_HEADS * MLA_BLK // MXU_DIM):
        sl = slice(c * MXU_DIM, (c + 1) * MXU_DIM)
        yac = ya[:, sl]
        ss = _dot((yac * yac).astype(BF16), onesq)
        r = lax.rsqrt(ss * invn_ref[:, sl] + EPS)
        qc = r * (yac * ga_ref[:, sl] * cos2 + yb[:, sl] * gb_ref[:, sl] * sin2)
        qcat_ref[:, sl] = qc.astype(BF16)
        kkc = kk[:, sl]
        ssk = _dot((kkc * kkc).astype(BF16), onesk)
        kn = kkc * lax.rsqrt(ssk * (1.0 / MLA_NOPE) + EPS) * gkn_ref[:, sl]
        kcat_ref[:, sl] = (kn + kr2).astype(BF16)
    v_ref[...] = _dot(ckv_b, wuv_ref[...]).astype(BF16)


def _head_block(rope_x1, rope_x2, nope):
    lead = (rope_x1 if rope_x1 is not None else nope).shape[:-1]
    z = lambda n: jnp.zeros(lead + (n,), F32)
    parts = [rope_x1 if rope_x1 is not None else z(ROPE_HALF),
             rope_x2 if rope_x2 is not None else z(ROPE_HALF),
             nope if nope is not None else z(MLA_NOPE),
             z(MLA_BLK - MLA_ROPE - MLA_NOPE)]
    return jnp.concatenate(parts, axis=-1)


def _mla_project(x, pos, gmix, w_down, g_qa, g_kva, w_uq, w_uk, w_uv, g_qn, g_qr, g_kn, g_kr):
    t = x.shape[0]
    tm = _pick_tile(t)
    hb = MLA_HEADS * MLA_BLK
    inv = ROPE_THETA ** (-jnp.arange(0, MLA_ROPE, 2, dtype=F32) / MLA_ROPE)
    ang = pos.astype(F32)[:, None] * inv[None, :]
    cs, sn = jnp.cos(ang), jnp.sin(ang)
    cos_t = _head_block(cs, cs, jnp.ones((t, MLA_NOPE), F32))
    sin_t = _head_block(-sn, sn, None)
    wd = w_down.astype(F32)
    base = MLA_Q_LORA + MLA_KV_LORA
    kx1, kx2 = wd[:, base:base + ROPE_HALF], wd[:, base + ROPE_HALF:base + MLA_ROPE]
    wd_ext = jnp.concatenate([wd[:, :base], _head_block(kx1, kx2, None), _head_block(kx2, kx1, None)],
                             axis=1).astype(BF16)
    g1, g2 = g_kr[:ROPE_HALF].astype(F32), g_kr[ROPE_HALF:].astype(F32)
    gkr_a = _head_block(g1, g2, None).reshape(1, MLA_BLK)
    gkr_b = _head_block(g2, g1, None).reshape(1, MLA_BLK)
    wq = w_uq.astype(F32).reshape(MLA_Q_LORA, MLA_HEADS, MLA_NOPE + MLA_ROPE)
    qn_w, q1_w, q2_w = wq[..., :MLA_NOPE], wq[..., MLA_NOPE:MLA_NOPE + ROPE_HALF], wq[..., MLA_NOPE + ROPE_HALF:]
    wuq_a = _head_block(q1_w, q2_w, qn_w).reshape(MLA_Q_LORA, hb).astype(BF16)
    wuq_b = _head_block(q2_w, q1_w, None).reshape(MLA_Q_LORA, hb).astype(BF16)
    r1, r2 = g_qr[:ROPE_HALF].astype(F32), g_qr[ROPE_HALF:].astype(F32)
    ga = jnp.tile(_head_block(r1, r2, g_qn.astype(F32)), MLA_HEADS).reshape(1, hb) * MLA_SCALE
    gb = jnp.tile(_head_block(r2, r1, None), MLA_HEADS).reshape(1, hb) * MLA_SCALE
    invn = jnp.tile(_head_block(jnp.full((ROPE_HALF,), 1.0 / MLA_ROPE, F32),
                                jnp.full((ROPE_HALF,), 1.0 / MLA_ROPE, F32),
                                jnp.full((MLA_NOPE,), 1.0 / MLA_NOPE, F32)), MLA_HEADS).reshape(1, hb)
    blk_groups = lambda groups: [(a + o, b + o) for o in range(0, MXU_DIM, MLA_BLK) for a, b in groups]
    ones_q = _group_ones(MXU_DIM, blk_groups([(0, MLA_ROPE), (MLA_ROPE, MLA_ROPE + MLA_NOPE)]))
    ones_k = _group_ones(MXU_DIM, blk_groups([(MLA_ROPE, MLA_ROPE + MLA_NOPE)]))
    wuk = _head_block(None, None, w_uk.astype(F32)).reshape(MLA_KV_LORA, hb).astype(BF16)
    gkn = jnp.tile(_head_block(None, None, g_kn.astype(F32)), MLA_HEADS).reshape(1, hb)
    wuv = w_uv.reshape(MLA_KV_LORA, MLA_HEADS * MLA_V).astype(BF16)

    row = lambda n: pl.BlockSpec((tm, n), lambda i: (i, 0))
    consts = [gmix.reshape(1, D_MODEL).astype(F32)]
    args = [x, consts[0], cos_t, sin_t, wd_ext,
            g_qa.reshape(1, -1).astype(F32), g_kva.reshape(1, -1).astype(F32), gkr_a, gkr_b,
            wuq_a, wuq_b, ga, gb, invn, ones_q, wuk, gkn, ones_k, wuv]
    in_specs = [row(D_MODEL), _const_spec((1, D_MODEL)), row(MLA_BLK), row(MLA_BLK)]
    in_specs += [_const_spec(a.shape) for a in args[4:]]
    return pl.pallas_call(
        _mla_proj_kernel,
        grid=(t // tm,),
        in_specs=in_specs,
        out_specs=[row(hb), row(hb), row(MLA_HEADS * MLA_V), row(MLA_KV_LORA), row(MLA_BLK)],
        out_shape=[jax.ShapeDtypeStruct((t, hb), BF16), jax.ShapeDtypeStruct((t, hb), BF16),
                   jax.ShapeDtypeStruct((t, MLA_HEADS * MLA_V), BF16),
                   jax.ShapeDtypeStruct((t, MLA_KV_LORA), F32),
                   jax.ShapeDtypeStruct((t, MLA_BLK), F32)],
        compiler_params=_params(("parallel",)),
        name="mla_proj",
    )(*args)


def _mla_attn_kernel(q_ref, k_ref, v_ref, o_ref, m_sc, l_sc, acc_sc, *, tq):
    i = pl.program_id(2)
    rows = 2 * tq
    m_sc[...] = jnp.full((rows, LANES), NEG, F32)
    l_sc[...] = jnp.zeros((rows, LANES), F32)
    acc_sc[...] = jnp.zeros((rows, LANES), F32)

    def step(j, masked):
        off = pl.multiple_of(j * tq, tq)
        s = jnp.concatenate(
            [_dot_nt(q_ref[:, a * LANES:(a + 1) * LANES], k_ref[pl.ds(off, tq), a * LANES:(a + 1) * LANES])
             for a in range(2)], axis=0)
        if masked:
            r = lax.broadcasted_iota(jnp.int32, (rows, tq), 0) & (tq - 1)
            c = lax.broadcasted_iota(jnp.int32, (rows, tq), 1)
            s = jnp.where(c <= r, s, NEG)
        m_prev = m_sc[...]
        m_new = jnp.maximum(m_prev, jnp.max(s, axis=-1, keepdims=True))
        alpha = jnp.exp(m_prev - m_new)
        p = jnp.exp(s - jnp.tile(m_new, (1, tq // LANES)))
        l_sc[...] = alpha * l_sc[...] + jnp.sum(p, axis=-1, keepdims=True)
        m_sc[...] = m_new
        acc_sc[...] = alpha * acc_sc[...] + _dot(p.astype(BF16), v_ref[pl.ds(off, tq), :])

    def body(j, carry):
        step(j, False)
        return carry

    lax.fori_loop(0, i, body, 0)
    step(i, True)
    a = acc_sc[...] / l_sc[...]
    lane = lax.broadcasted_iota(jnp.int32, (tq, LANES), 1)
    o_ref[...] = jnp.where(lane < MLA_V, a[:tq], a[tq:]).astype(BF16)


def _mla_attend_prompt(qcat, kcat, v, batch, seq):
    tq = _pick_tile(seq, (512, 256, 128))
    nq = seq // tq
    kern = functools.partial(_mla_attn_kernel, tq=tq)
    return pl.pallas_call(
        kern,
        grid=(batch, MLA_HEADS // 2, nq),
        in_specs=[pl.BlockSpec((tq, 2 * MLA_BLK), lambda b, h, i: (b * nq + i, h)),
                  pl.BlockSpec((seq, 2 * MLA_BLK), lambda b, h, i: (b, h)),
                  pl.BlockSpec((seq, 2 * MLA_V), lambda b, h, i: (b, h))],
        out_specs=pl.BlockSpec((tq, 2 * MLA_V), lambda b, h, i: (b * nq + i, h)),
        out_shape=jax.ShapeDtypeStruct((batch * seq, MLA_HEADS * MLA_V), BF16),
        scratch_shapes=[pltpu.VMEM((2 * tq, LANES), F32)] * 3,
        compiler_params=_params(("parallel", "parallel", "arbitrary")),
        name="mla_attn_prompt",
    )(qcat, kcat, v)


def _mla_sample_kernel(pt_ref, qn_ref, qr_ref, cn_ref, rn_ref, wuk_ref, gkn_ref, ones_ref, wuv_ref, *rest, n_pp):
    c_pages = rest[:n_pp]
    r_pages = rest[n_pp:2 * n_pp]
    o_ref = rest[2 * n_pp]
    cbuf, rbuf, m_sc, l_sc, acc_sc = rest[2 * n_pp + 1:]
    del pt_ref
    c = pl.program_id(1)

    @pl.when(c == 0)
    def _():
        m_sc[...] = jnp.full(m_sc.shape, NEG, F32)
        l_sc[...] = jnp.zeros(l_sc.shape, F32)
        acc_sc[...] = jnp.zeros(acc_sc.shape, F32)

    def scores(cb, rb):
        kk = _dot(cb, wuk_ref[...])
        parts = []
        for j in range(MLA_HEADS * MLA_NOPE // MXU_DIM):
            sl = slice(j * MXU_DIM, (j + 1) * MXU_DIM)
            kkc = kk[:, sl]
            ss = _dot((kkc * kkc).astype(BF16), ones_ref[...])
            parts.append((kkc * lax.rsqrt(ss * (1.0 / MLA_NOPE) + EPS) * gkn_ref[:, sl]).astype(BF16))
        kn = jnp.concatenate(parts, axis=1)
        return _dot(kn, qn_ref[...]) + _dot(rb, qr_ref[...])

    for r in range(n_pp):
        cbuf[r * PAGE_SIZE:(r + 1) * PAGE_SIZE, :] = c_pages[r][...].astype(BF16)
        rbuf[r * PAGE_SIZE:(r + 1) * PAGE_SIZE, :] = r_pages[r][...].astype(BF16)
    _softmax_update_cols(scores(cbuf[...], rbuf[...]), cbuf[...], m_sc, l_sc, acc_sc)

    @pl.when(c == pl.num_programs(1) - 1)
    def _():
        row = lax.broadcasted_iota(jnp.int32, (LANES, LANES), 0)
        col = lax.broadcasted_iota(jnp.int32, (LANES, LANES), 1)
        cn = cn_ref[...].astype(BF16)
        s = jnp.where(row <= (col >> 3), scores(cn, rn_ref[...].astype(BF16)), NEG)
        _softmax_update_cols(s, cn, m_sc, l_sc, acc_sc)
        inv = jnp.tile(1.0 / _col_replicated(l_sc[...]), (1, MLA_KV_LORA // LANES))
        o_lat = (acc_sc[...] * inv).astype(BF16)
        o2 = _dot(o_lat, wuv_ref[...])
        hrow = lax.broadcasted_iota(jnp.int32, (MLA_HEADS, MLA_HEADS * MLA_V), 0)
        hcol = lax.broadcasted_iota(jnp.int32, (MLA_HEADS, MLA_HEADS * MLA_V), 1) // MLA_V
        keep = hrow == hcol
        outs = [jnp.sum(jnp.where(keep, o2[t * MLA_HEADS:(t + 1) * MLA_HEADS, :], 0.0), axis=0, keepdims=True)
                for t in range(4)]
        o_ref[...] = jnp.concatenate(outs, axis=0)


def _mla_attend_sample(qcat_s, ckv_s, kr_s, cache_ckv, cache_kr, page_table, w_uk, g_kn, w_uv, layer):
    n_new = 4
    n_seq = qcat_s.shape[0] // n_new
    n_pages = page_table.shape[1]
    n_pp = min(PAGES_PER_STEP, n_pages)
    n_chunks = n_pages // n_pp
    nw = MLA_HEADS * MLA_NOPE
    q4 = qcat_s.reshape(n_seq, n_new, MLA_HEADS, MLA_BLK)
    q_rope = q4[..., :MLA_ROPE]
    q_nope = q4[..., MLA_ROPE:MLA_ROPE + MLA_NOPE]
    eye_h = jnp.eye(MLA_HEADS, dtype=BF16)
    qn_bd = jnp.einsum('bthd,hH->bhdtH', q_nope, eye_h).reshape(n_seq, nw, n_new * MLA_HEADS)
    qn_bd = jnp.pad(qn_bd, ((0, 0), (0, 0), (0, LANES - n_new * MLA_HEADS)))
    qr = q_rope.transpose(0, 3, 1, 2).reshape(n_seq, MLA_ROPE, n_new * MLA_HEADS)
    qr = jnp.pad(qr, ((0, 0), (0, 0), (0, LANES - n_new * MLA_HEADS)))
    pad_rows = lambda a, w: jnp.pad(a.reshape(n_seq, n_new, w), ((0, 0), (0, LANES - n_new), (0, 0)))
    c_new = pad_rows(ckv_s, MLA_KV_LORA)
    r_new = pad_rows(kr_s, MLA_ROPE)
    ones = _group_ones(MXU_DIM, [(i * MLA_NOPE, (i + 1) * MLA_NOPE) for i in range(MXU_DIM // MLA_NOPE)])

    def page_spec(r, width):
        return pl.BlockSpec((None, None, PAGE_SIZE, width),
                            lambda b, c, pt: (layer, pt[b * n_pages + c * n_pp + r], 0, 0))

    per_seq = lambda shape: pl.BlockSpec((None,) + shape, lambda b, c, pt: (b,) + (0,) * len(shape))
    const = lambda shape: pl.BlockSpec(shape, lambda b, c, pt: (0,) * len(shape))
    kern = functools.partial(_mla_sample_kernel, n_pp=n_pp)
    out = pl.pallas_call(
        kern,
        grid_spec=pltpu.PrefetchScalarGridSpec(
            num_scalar_prefetch=1,
            grid=(n_seq, n_chunks),
            in_specs=[per_seq((nw, LANES)), per_seq((MLA_ROPE, LANES)), per_seq((LANES, MLA_KV_LORA)),
                      per_seq((LANES, MLA_ROPE)), const((MLA_KV_LORA, nw)), const((1, nw)),
                      const(ones.shape), const((MLA_KV_LORA, MLA_HEADS * MLA_V))]
                     + [page_spec(r, MLA_KV_LORA) for r in range(n_pp)]
                     + [page_spec(r, MLA_ROPE) for r in range(n_pp)],
            out_specs=per_seq((n_new, MLA_HEADS * MLA_V)),
            scratch_shapes=[pltpu.VMEM((n_pp * PAGE_SIZE, MLA_KV_LORA), BF16),
                            pltpu.VMEM((n_pp * PAGE_SIZE, MLA_ROPE), BF16),
                            pltpu.VMEM((1, LANES), F32), pltpu.VMEM((1, LANES), F32),
                            pltpu.VMEM((LANES, MLA_KV_LORA), F32)]),
        out_shape=jax.ShapeDtypeStruct((n_seq, n_new, MLA_HEADS * MLA_V), F32),
        compiler_params=_params(("parallel", "arbitrary")),
        name="mla_attn_sample",
    )(page_table.reshape(-1), qn_bd, qr, c_new, r_new,
      w_uk.reshape(MLA_KV_LORA, nw).astype(BF16),
      jnp.tile(g_kn.astype(F32), MLA_HEADS).reshape(1, nw), ones,
      w_uv.reshape(MLA_KV_LORA, MLA_HEADS * MLA_V).astype(BF16),
      *([cache_ckv] * n_pp), *([cache_kr] * n_pp))
    return out.reshape(n_seq * n_new, MLA_HEADS * MLA_V).astype(BF16)


def _hg_proj_kernel(x_ref, gmix_ref, w_ref, lbl_ref, q_ref, k_ref, lf_ref, v_ref, g_ref, *, layer):
    h = _rms(x_ref[...], gmix_ref[...]).astype(BF16)
    z = _dot(h, w_ref[...])
    lg = lbl_ref[...]
    e = jnp.exp(lg - jnp.max(lg, axis=0, keepdims=True))
    sm = e / jnp.sum(e, axis=0, keepdims=True)
    lb = jnp.sum(sm[:layer + 1], axis=0, keepdims=True) - sm[0:1]
    nk = HG_HEADS * HG_DK
    zq = z[:, :nk]
    q_ref[...] = zq * jax.nn.sigmoid(zq)
    f = lb + (1.0 - lb) * jax.nn.sigmoid(z[:, nk:2 * nk])
    k_ref[...] = 1.0 - f
    lf_ref[...] = jnp.log(f)
    v_ref[...] = z[:, 2 * nk:3 * nk]
    g_ref[...] = z[:, 3 * nk:]


def _hg_project(x, gmix, w_in, lb_logits, layer):
    t = x.shape[0]
    tm = _pick_tile(t)
    nk = HG_HEADS * HG_DK
    row = lambda n: pl.BlockSpec((tm, n), lambda i: (i, 0))
    kern = functools.partial(_hg_proj_kernel, layer=layer)
    return pl.pallas_call(
        kern,
        grid=(t // tm,),
        in_specs=[row(D_MODEL), _const_spec((1, D_MODEL)), _const_spec(w_in.shape), _const_spec(lb_logits.shape)],
        out_specs=[row(nk)] * 5,
        out_shape=[jax.ShapeDtypeStruct((t, nk), F32)] * 5,
        compiler_params=_params(("parallel",)),
        name="hg_proj",
    )(x, gmix.reshape(1, D_MODEL).astype(F32), w_in.astype(BF16), lb_logits.astype(F32))


def _hg_out(o, g, gout):
    return (_rms(o, gout) * (g * jax.nn.sigmoid(g))).astype(BF16)


def _hg_chunk_kernel(q_ref, k_ref, lf_ref, v_ref, g_ref, gout_ref, o_ref, sfin_ref, s_sc, cum_sc, o_sc, *, chunk):
    n = pl.program_id(1)

    @pl.when(n == 0)
    def _():
        s_sc[...] = jnp.zeros(s_sc.shape, F32)

    lf = lf_ref[...]
    r_i = lax.broadcasted_iota(jnp.int32, (chunk, chunk), 0)
    c_i = lax.broadcasted_iota(jnp.int32, (chunk, chunk), 1)
    tri = jnp.where(c_i <= r_i, 1.0, 0.0).astype(BF16)
    hi = lf.astype(BF16)
    r1 = lf - hi.astype(F32)
    mid = r1.astype(BF16)
    lo = (r1 - mid.astype(F32)).astype(BF16)
    cum_all = _dot(tri, hi) + _dot(tri, mid) + _dot(tri, lo)
    cum_sc[...] = cum_all

    nsub = chunk // HG_SUB
    half = HG_SUB // 2
    half_row = lax.broadcasted_iota(jnp.int32, (half, LANES), 0)
    for h in range(HG_HEADS):
        sl = slice(h * HG_DK, (h + 1) * HG_DK)
        q = q_ref[:, sl]
        k = k_ref[:, sl]
        v = v_ref[:, sl]
        vb = v.astype(BF16)
        cu = cum_all[:, sl]
        state = s_sc[h]
        o_sc[:, sl] = _dot((q * jnp.exp(cu)).astype(BF16), state.astype(BF16))
        size = chunk // 2
        while size >= HG_SUB:
            for start in range(0, chunk, 2 * size):
                mid_r = start + size
                ref_row = cu[mid_r - 1:mid_r, :]
                qa = (q[mid_r:mid_r + size] * jnp.exp(cu[mid_r:mid_r + size] - ref_row)).astype(BF16)
                ka = (k[start:mid_r] * jnp.exp(ref_row - cu[start:mid_r])).astype(BF16)
                a = _dot_nt(qa, ka).astype(BF16)
                o_sc[mid_r:mid_r + size, sl] += _dot(a, vb[start:mid_r])
            size //= 2

        def diag_block(b, carry, sl=sl):
            rows = pl.ds(pl.multiple_of(b * HG_SUB, HG_SUB), HG_SUB)
            qb, kb, vv, cb = q_ref[rows, sl], k_ref[rows, sl], v_ref[rows, sl], cum_sc[rows, sl]
            acc = [jnp.zeros((half, LANES), F32), jnp.zeros((half, LANES), F32)]
            for s in range(HG_SUB):
                ks, vs, cs = kb[s:s + 1], vv[s:s + 1], cb[s:s + 1]
                for part in range(s // half, 2):
                    rs = slice(part * half, (part + 1) * half)
                    e = cb[rs] - cs
                    if part == s // half:
                        e = jnp.where(half_row >= s - part * half, e, NEG)
                    w = qb[rs] * ks * jnp.exp(e)
                    acc[part] = acc[part] + jnp.sum(w, axis=-1, keepdims=True) * vs
            o_sc[rows, sl] += jnp.concatenate(acc, axis=0)
            return carry

        lax.fori_loop(0, nsub, diag_block, 0)
        o_ref[:, sl] = _hg_out(o_sc[:, sl], g_ref[:, sl], gout_ref[...])
        last = cu[chunk - 1:chunk, :]
        k_dec = (k * jnp.exp(last - cu)).T.astype(BF16)
        s_sc[h] = state * _col_replicated(jnp.exp(last)) + _dot(k_dec, vb)

    @pl.when(n == pl.num_programs(1) - 1)
    def _():
        sfin_ref[...] = s_sc[...]


def _hg_recur_prompt(q, k, lf, v, g, g_out, batch, seq):
    chunk = HG_CHUNK
    nc = seq // chunk
    nk = HG_HEADS * HG_DK
    blk = pl.BlockSpec((chunk, nk), lambda b, n: (b * nc + n, 0))
    kern = functools.partial(_hg_chunk_kernel, chunk=chunk)
    return pl.pallas_call(
        kern,
        grid=(batch, nc),
        in_specs=[blk] * 5 + [pl.BlockSpec((1, HG_DV), lambda b, n: (0, 0))],
        out_specs=[blk, pl.BlockSpec((None, HG_HEADS, HG_DK, HG_DV), lambda b, n: (b, 0, 0, 0))],
        out_shape=[jax.ShapeDtypeStruct((batch * seq, nk), BF16),
                   jax.ShapeDtypeStruct((batch, HG_HEADS, HG_DK, HG_DV), F32)],
        scratch_shapes=[pltpu.VMEM((HG_HEADS, HG_DK, HG_DV), F32), pltpu.VMEM((chunk, nk), F32),
                        pltpu.VMEM((chunk, nk), F32)],
        compiler_params=_params(("parallel", "arbitrary")),
        name="hg_recur_prompt",
    )(q, k, lf, v, g, g_out.reshape(1, HG_DV).astype(F32))


def _hg_sample_kernel(q_ref, k_ref, lf_ref, v_ref, g_ref, gout_ref, s0_ref, o_ref, s_ref, *, n_new, n_blk):
    row8 = lax.broadcasted_iota(jnp.int32, (8, LANES), 0)

    def rows_to_tile(rows):
        top = jnp.zeros((8, LANES), F32)
        for t, r in enumerate(rows):
            top = jnp.where(row8 == t, r, top)
        return jnp.concatenate([top, jnp.zeros((LANES - 8, LANES), F32)], axis=0)

    def per_seq(b, carry):
        for h in range(HG_HEADS):
            sl = slice(h * HG_DK, (h + 1) * HG_DK)
            q = [q_ref[b, t:t + 1, sl] for t in range(n_new)]
            k = [k_ref[b, t:t + 1, sl] for t in range(n_new)]
            v = [v_ref[b, t:t + 1, sl] for t in range(n_new)]
            cum = []
            for t in range(n_new):
                lf_t = lf_ref[b, t:t + 1, sl]
                cum.append(lf_t if t == 0 else cum[-1] + lf_t)
            s0 = s0_ref[b, h]
            q_dec = rows_to_tile([q[t] * jnp.exp(cum[t]) for t in range(n_new)])[:16]
            inter = _dot(q_dec.astype(BF16), s0.astype(BF16))
            for t in range(n_new):
                o = inter[t:t + 1]
                for s in range(t + 1):
                    a = jnp.sum(q[t] * k[s] * jnp.exp(cum[t] - cum[s]), axis=-1, keepdims=True)
                    o = o + a * v[s]
                o_ref[b, t:t + 1, sl] = _rms(o, gout_ref[...]) * (
                    g_ref[b, t:t + 1, sl] * jax.nn.sigmoid(g_ref[b, t:t + 1, sl]))
            last = cum[-1]
            k_dec = rows_to_tile([k[s] * jnp.exp(last - cum[s]) for s in range(n_new)])
            v_tile = rows_to_tile(v)
            s_ref[b, h] = (s0 * _col_replicated(jnp.exp(last))
                           + _dot(k_dec.T.astype(BF16), v_tile.astype(BF16)))
        return carry

    lax.fori_loop(0, n_blk, per_seq, 0)


def _hg_recur_sample(q, k, lf, v, g, g_out, state):
    n_seq, n_new, nk = q.shape
    n_blk = _pick_tile(n_seq, (HG_SEQ_BLOCK, 4, 2, 1))
    tok = pl.BlockSpec((n_blk, n_new, nk), lambda i: (i, 0, 0))
    st = pl.BlockSpec((n_blk, HG_HEADS, HG_DK, HG_DV), lambda i: (i, 0, 0, 0))
    kern = functools.partial(_hg_sample_kernel, n_new=n_new, n_blk=n_blk)
    return pl.pallas_call(
        kern,
        grid=(n_seq // n_blk,),
        in_specs=[tok] * 5 + [pl.BlockSpec((1, HG_DV), lambda i: (0, 0)), st],
        out_specs=[tok, st],
        out_shape=[jax.ShapeDtypeStruct((n_seq, n_new, nk), F32),
                   jax.ShapeDtypeStruct(state.shape, F32)],
        compiler_params=_params(("parallel",)),
        name="hg_recur_sample",
    )(q, k, lf, v, g, g_out.reshape(1, HG_DV).astype(F32), state.astype(F32))


def kernel(x_prompt, x_sample, cache_da_k, cache_da_v, cache_mla_ckv, cache_mla_krope, state_hgrn, page_table, norm_mix, norm_ffn, ffn_w_gu, ffn_w_down, da_w_qkv, da_q_gain, da_k_gain, da_lambda, da_sub_gain, da_w_o, mla_w_down, mla_qa_gain, mla_kva_gain, mla_w_uq, mla_w_uk, mla_w_uv, mla_qn_gain, mla_qr_gain, mla_kn_gain, mla_kr_gain, mla_w_o, hg_w_in, hg_lb_logits, hg_out_gain, hg_w_o):
    batch, seq, _ = x_prompt.shape
    n_seq, n_new, _ = x_sample.shape
    assert n_new == 4 and seq % HG_CHUNK == 0
    tp, ts = batch * seq, n_seq * n_new
    past = page_table.shape[1] * PAGE_SIZE
    depth = norm_mix.shape[0]
    x = jnp.concatenate([x_prompt.reshape(tp, D_MODEL), x_sample.reshape(ts, D_MODEL)], axis=0).astype(F32)
    pos = jnp.concatenate([jnp.tile(jnp.arange(seq), batch), jnp.tile(past + jnp.arange(n_new), n_seq)])

    da_kp, da_vp, da_ks, da_vs = [], [], [], []
    mla_cp, mla_rp, mla_cs, mla_rs = [], [], [], []
    hg_p, hg_s = [], []
    for i in range(depth):
        kind, j = i % N_MIXERS, i // N_MIXERS
        if kind == 0:
            lam_init = 0.8 - 0.6 * math.exp(-0.3 * i)
            lam_p = da_lambda[j].astype(F32)
            q, kf, vf, kb, vb = _da_project(x, norm_mix[i], da_w_qkv[j], da_q_gain[j], da_k_gain[j])
            a_p = _da_attend_prompt(q, kb, vb, lam_p, da_sub_gain[j], lam_init, batch, seq)
            a_s = _da_attend_sample(q[tp:], kb[tp:], vb[tp:], cache_da_k, cache_da_v, page_table,
                                    lam_p, da_sub_gain[j], lam_init, j)
            w_o = da_w_o[j]
            da_kp.append(kf[:tp].reshape(batch, seq, DA_KV_HEADS, 2, DA_HEAD))
            da_vp.append(vf[:tp].reshape(batch, seq, DA_KV_HEADS, 2 * DA_HEAD))
            da_ks.append(kf[tp:].reshape(n_seq, n_new, DA_KV_HEADS, 2, DA_HEAD))
            da_vs.append(vf[tp:].reshape(n_seq, n_new, DA_KV_HEADS, 2 * DA_HEAD))
        elif kind == 1:
            qcat, kcat, v, ckv, kr = _mla_project(
                x, pos, norm_mix[i], mla_w_down[j], mla_qa_gain[j], mla_kva_gain[j], mla_w_uq[j],
                mla_w_uk[j], mla_w_uv[j], mla_qn_gain[j], mla_qr_gain[j], mla_kn_gain[j], mla_kr_gain[j])
            kr = kr[:, :MLA_ROPE]
            a_p = _mla_attend_prompt(qcat, kcat, v, batch, seq)
            a_s = _mla_attend_sample(qcat[tp:], ckv[tp:], kr[tp:], cache_mla_ckv, cache_mla_krope,
                                     page_table, mla_w_uk[j], mla_kn_gain[j], mla_w_uv[j], j)
            w_o = mla_w_o[j]
            mla_cp.append(ckv[:tp].reshape(batch, seq, MLA_KV_LORA))
            mla_rp.append(kr[:tp].reshape(batch, seq, MLA_ROPE))
            mla_cs.append(ckv[tp:].reshape(n_seq, n_new, MLA_KV_LORA))
            mla_rs.append(kr[tp:].reshape(n_seq, n_new, MLA_ROPE))
        else:
            hq, hk, hlf, hv, hgate = _hg_project(x, norm_mix[i], hg_w_in[j], hg_lb_logits, i)
            a_p, s_p = _hg_recur_prompt(hq, hk, hlf, hv, hgate, hg_out_gain[j], batch, seq)
            smp = lambda a: a[tp:].reshape(n_seq, n_new, HG_HEADS * HG_DK)
            a_s, s_s = _hg_recur_sample(smp(hq), smp(hk), smp(hlf), smp(hv), smp(hgate),
                                        hg_out_gain[j], state_hgrn[j])
            a_s = a_s.reshape(ts, HG_HEADS * HG_DV).astype(BF16)
            w_o = hg_w_o[j]
            hg_p.append(s_p.astype(state_hgrn.dtype))
            hg_s.append(s_s.astype(state_hgrn.dtype))
        a = jnp.concatenate([a_p, a_s], axis=0)
        x = _post(x, a, w_o, norm_ffn[i], ffn_w_gu[i], ffn_w_down[i])

    return (x[:tp].reshape(batch, seq, D_MODEL), x[tp:].reshape(n_seq, n_new, D_MODEL),
            jnp.stack(da_kp), jnp.stack(da_vp), jnp.stack(da_ks), jnp.stack(da_vs),
            jnp.stack(mla_cp), jnp.stack(mla_rp), jnp.stack(mla_cs), jnp.stack(mla_rs),
            jnp.stack(hg_p), jnp.stack(hg_s))
```

```python
import functools
import math

import jax
import jax.numpy as jnp
from jax import lax
from jax.experimental import pallas as pl
from jax.experimental.pallas import tpu as pltpu

F32 = jnp.float32
BF16 = jnp.bfloat16

D_MODEL = 1024
PAGE_SIZE = 128
N_MIXERS = 3
N_NEW = 4
DA_HEADS = 8
DA_KV_HEADS = 4
DA_GROUP = DA_HEADS // DA_KV_HEADS
DA_HEAD = 64
MLA_HEADS = 8
MLA_NOPE = 64
MLA_ROPE = 32
MLA_V = 64
MLA_Q_LORA = 256
MLA_KV_LORA = 256
MLA_SCALE = (MLA_NOPE + MLA_ROPE) ** -0.5
ROPE_THETA = 10000.0
HG_HEADS = 8
HG_DK = 128
HG_DV = 128
D_FF = 2816
EPS = 1e-6

LOG2E = math.log2(math.e)
NEG = -1e30
LANES = 128
MXU_DIM = 256
VMEM_LIMIT = 56 * 1024 * 1024
PAGES_PER_STEP = 16
HG_CHUNK = 128
HG_SEQ_BLOCK = 8


def _pick_tile(n, candidates=(512, 256, 128, 64, 32, 16, 8)):
    for c in candidates:
        if n % c == 0:
            return c
    raise ValueError(f"no tile divides {n}")


def _const_spec(shape):
    nd = len(shape)
    return pl.BlockSpec(shape, lambda *_: (0,) * nd, pipeline_mode=pl.Buffered(1))


def _params(sem):
    return pltpu.CompilerParams(dimension_semantics=sem, vmem_limit_bytes=VMEM_LIMIT)


def _rms(x, g):
    ms = jnp.mean(x * x, axis=-1, keepdims=True)
    return x * lax.rsqrt(ms + EPS) * g


def _dot(a, b):
    return jnp.dot(a, b, preferred_element_type=F32)


def _dot_nt(a, b):
    return lax.dot_general(a, b, (((1,), (1,)), ((), ())), preferred_element_type=F32)


def _group_ones(width, groups):
    idx = jnp.arange(width)
    gid = jnp.full((width,), -1, jnp.int32)
    for n, (a, b) in enumerate(groups):
        gid = jnp.where((idx >= a) & (idx < b), n, gid)
    m = (gid[:, None] == gid[None, :]) & (gid[:, None] >= 0)
    return m.astype(BF16)


def _lambda_value(lv, lam_init):
    a = jnp.sum(lv[0:1] * lv[1:2], axis=-1, keepdims=True)
    b = jnp.sum(lv[2:3] * lv[3:4], axis=-1, keepdims=True)
    return jnp.exp(a) - jnp.exp(b) + lam_init


def _col_replicated(row):
    return jnp.broadcast_to(row, (LANES, LANES)).T


def _softmax_init(m_sc, l_sc, acc_sc):
    m_sc[...] = jnp.full(m_sc.shape, NEG, F32)
    l_sc[...] = jnp.zeros(l_sc.shape, F32)
    acc_sc[...] = jnp.zeros(acc_sc.shape, F32)


def _softmax_update(s, vb, m_sc, l_sc, acc_sc):
    n = s.shape[1]
    m_prev = m_sc[...]
    m_new = jnp.maximum(m_prev, jnp.max(s, axis=-1, keepdims=True))
    alpha = jnp.exp2(m_prev - m_new)
    p = jnp.exp2(s - jnp.tile(m_new, (1, n // LANES)))
    l_sc[...] = alpha * l_sc[...] + jnp.sum(p, axis=-1, keepdims=True)
    m_sc[...] = m_new
    acc_sc[...] = (jnp.tile(alpha, (1, acc_sc.shape[1] // LANES)) * acc_sc[...]
                   + _dot(p.astype(BF16), vb))


def _da_proj_kernel(x_ref, gmix_ref, w_ref, gq_ref, gk_ref, ones_ref,
                    q_ref, kf_ref, vf_ref, kb_ref, vb_ref, *, cache_layout):
    tm = x_ref.shape[0]
    h = _rms(x_ref[...], gmix_ref[...]).astype(BF16)
    y = _dot(h, w_ref[...])
    ones = ones_ref[...]

    def group_norm(yc, g):
        ss = _dot((yc * yc).astype(BF16), ones)
        return yc * lax.rsqrt(ss * (1.0 / DA_HEAD) + EPS) * g

    nq = DA_HEADS * 2 * DA_HEAD
    nk = DA_KV_HEADS * 2 * DA_HEAD
    for c in range(nq // MXU_DIM):
        sl = slice(c * MXU_DIM, (c + 1) * MXU_DIM)
        q_ref[:, sl] = group_norm(y[:, sl], gq_ref[:, sl]).astype(BF16)
    for c in range(nk // MXU_DIM):
        sl = slice(c * MXU_DIM, (c + 1) * MXU_DIM)
        kn = group_norm(y[:, nq + c * MXU_DIM:nq + (c + 1) * MXU_DIM], gk_ref[:, sl])
        if cache_layout:
            kf_ref[sl, :] = kn.T
        else:
            kf_ref[:, sl] = kn
        kb_ref[:, sl] = kn.astype(BF16)
    v = y[:, nq + nk:]
    if cache_layout:
        for hd in range(DA_KV_HEADS):
            vf_ref[pl.ds(hd, tm, stride=DA_KV_HEADS), :] = v[:, hd * LANES:(hd + 1) * LANES]
    else:
        vf_ref[...] = v
    vb_ref[...] = v.astype(BF16)


def _da_proj_consts(gmix, w_qkv, g_q, g_k):
    nq = DA_HEADS * 2 * DA_HEAD
    nk = DA_KV_HEADS * 2 * DA_HEAD
    gq_row = (jnp.tile(g_q.astype(F32), nq // DA_HEAD) * (DA_HEAD ** -0.5 * LOG2E)).reshape(1, nq)
    gk_row = jnp.tile(g_k.astype(F32), nk // DA_HEAD).reshape(1, nk)
    ones = _group_ones(MXU_DIM, [(i * DA_HEAD, (i + 1) * DA_HEAD) for i in range(MXU_DIM // DA_HEAD)])
    return (gmix.reshape(1, D_MODEL).astype(F32), w_qkv.astype(BF16), gq_row, gk_row, ones)


def _da_project(x, consts, seq=None):
    t = x.shape[0]
    tm = _pick_tile(t if seq is None else seq)
    nq = DA_HEADS * 2 * DA_HEAD
    nk = DA_KV_HEADS * 2 * DA_HEAD
    row = lambda n: pl.BlockSpec((tm, n), lambda i: (i, 0))
    if seq is None:
        kv_specs = [row(nk), row(nk)]
        kv_shapes = [jax.ShapeDtypeStruct((t, nk), F32), jax.ShapeDtypeStruct((t, nk), F32)]
    else:
        per = seq // tm
        kv_specs = [pl.BlockSpec((None, nk, tm), lambda i: (i // per, 0, i % per)),
                    pl.BlockSpec((DA_KV_HEADS * tm, 2 * DA_HEAD), lambda i: (i, 0))]
        kv_shapes = [jax.ShapeDtypeStruct((t // seq, nk, seq), F32),
                     jax.ShapeDtypeStruct((t * DA_KV_HEADS, 2 * DA_HEAD), F32)]
    kern = functools.partial(_da_proj_kernel, cache_layout=seq is not None)
    return pl.pallas_call(
        kern,
        grid=(t // tm,),
        in_specs=[row(D_MODEL)] + [_const_spec(c.shape) for c in consts],
        out_specs=[row(nq)] + kv_specs + [row(nk), row(nk)],
        out_shape=[jax.ShapeDtypeStruct((t, nq), BF16)] + kv_shapes
                  + [jax.ShapeDtypeStruct((t, nk), BF16), jax.ShapeDtypeStruct((t, nk), BF16)],
        compiler_params=_params(("parallel",)),
        name="da_proj",
    )(x, *consts)


def _da_attn_kernel(lam_ref, q_ref, k_ref, v_ref, gsub_ref, o_ref,
                    qs_sc, m_sc, l_sc, acc_sc, *, tq, lam_init):
    i = pl.program_id(2)
    rows = 2 * DA_GROUP * tq
    q = q_ref[...]
    lane = lax.broadcasted_iota(jnp.int32, (tq, LANES), 1)
    first_map = lane < DA_HEAD
    zero = jnp.zeros((tq, LANES), BF16)
    for g in range(DA_GROUP):
        qg = q[:, g * LANES:(g + 1) * LANES]
        qs_sc[(2 * g) * tq:(2 * g + 1) * tq, :] = jnp.where(first_map, qg, zero)
        qs_sc[(2 * g + 1) * tq:(2 * g + 2) * tq, :] = jnp.where(first_map, zero, qg)
    _softmax_init(m_sc, l_sc, acc_sc)

    def step(j, masked):
        off = pl.multiple_of(j * tq, tq)
        s = _dot_nt(qs_sc[...], k_ref[pl.ds(off, tq), :])
        if masked:
            r = lax.broadcasted_iota(jnp.int32, (rows, tq), 0) & (tq - 1)
            c = lax.broadcasted_iota(jnp.int32, (rows, tq), 1)
            s = jnp.where(c <= r, s, NEG)
        _softmax_update(s, v_ref[pl.ds(off, tq), :], m_sc, l_sc, acc_sc)

    def body(j, carry):
        step(j, False)
        return carry

    lax.fori_loop(0, i, body, 0)
    step(i, True)

    lam = _lambda_value(lam_ref[...], lam_init)
    a = acc_sc[...] / l_sc[...]
    for g in range(DA_GROUP):
        o = a[(2 * g) * tq:(2 * g + 1) * tq] - lam * a[(2 * g + 1) * tq:(2 * g + 2) * tq]
        on = _rms(o, gsub_ref[...]) * (1.0 - lam_init)
        o_ref[:, g * LANES:(g + 1) * LANES] = on.astype(BF16)


def _da_attend_prompt(q, kb, vb, lam_p, g_sub, lam_init, batch, seq):
    tq = _pick_tile(seq, (512, 256, 128))
    nq = seq // tq
    rows = 2 * DA_GROUP * tq
    kern = functools.partial(_da_attn_kernel, tq=tq, lam_init=lam_init)
    return pl.pallas_call(
        kern,
        grid=(batch, DA_KV_HEADS, nq),
        in_specs=[_const_spec(lam_p.shape),
                  pl.BlockSpec((tq, 2 * LANES), lambda b, h, i: (b * nq + i, h)),
                  pl.BlockSpec((seq, LANES), lambda b, h, i: (b, h)),
                  pl.BlockSpec((seq, LANES), lambda b, h, i: (b, h)),
                  _const_spec((1, LANES))],
        out_specs=pl.BlockSpec((tq, 2 * LANES), lambda b, h, i: (b * nq + i, h)),
        out_shape=jax.ShapeDtypeStruct((batch * seq, DA_HEADS * 2 * DA_HEAD), BF16),
        scratch_shapes=[pltpu.VMEM((rows, LANES), BF16), pltpu.VMEM((rows, LANES), F32),
                        pltpu.VMEM((rows, LANES), F32), pltpu.VMEM((rows, LANES), F32)],
        compiler_params=_params(("parallel", "parallel", "arbitrary")),
        name="da_attn_prompt",
    )(lam_p, q, kb, vb, g_sub.reshape(1, LANES).astype(F32))


def _da_sample_kernel(pt_ref, lam_ref, qbd_ref, kn_ref, vn_ref, gsub_ref, *rest, n_pp, lam_init):
    k_pages = rest[:n_pp]
    v_pages = rest[n_pp:2 * n_pp]
    o_ref = rest[2 * n_pp]
    kbuf, vbuf, m_sc, l_sc, acc_sc = rest[2 * n_pp + 1:]
    del pt_ref
    c = pl.program_id(1)

    @pl.when(c == 0)
    def _():
        _softmax_init(m_sc, l_sc, acc_sc)

    qbd = qbd_ref[...]
    for r in range(n_pp):
        cols = slice(r * PAGE_SIZE, (r + 1) * PAGE_SIZE)
        kbuf[:, cols] = k_pages[r][...].astype(BF16)
        for h in range(DA_KV_HEADS):
            vbuf[cols, h * LANES:(h + 1) * LANES] = (
                v_pages[r][pl.ds(h, PAGE_SIZE, stride=DA_KV_HEADS), :].astype(BF16))
    _softmax_update(_dot(qbd, kbuf[...]), vbuf[...], m_sc, l_sc, acc_sc)

    @pl.when(c == pl.num_programs(1) - 1)
    def _():
        n_rows = qbd.shape[0]
        row = lax.broadcasted_iota(jnp.int32, (n_rows, LANES), 0)
        col = lax.broadcasted_iota(jnp.int32, (n_rows, LANES), 1)
        t_row = (row >> 1) & (N_NEW - 1)
        s = jnp.where(col <= t_row, _dot(qbd, kn_ref[...]), NEG)
        _softmax_update(s, vn_ref[...], m_sc, l_sc, acc_sc)

        lam = _lambda_value(lam_ref[...], lam_init)
        inv = 1.0 / l_sc[...]
        acc = acc_sc[...]
        half = N_NEW * DA_GROUP
        for h in range(DA_KV_HEADS):
            rows = slice(h * 2 * half, (h + 1) * 2 * half)
            blk = acc[rows, h * LANES:(h + 1) * LANES] * inv[rows, :]
            o = blk[:half] - lam * blk[half:]
            o_ref[h] = _rms(o, gsub_ref[...]) * (1.0 - lam_init)


def _da_attend_sample(q_s, kb_s, vb_s, cache_k, cache_v, page_table, lam_p, g_sub, lam_init, layer):
    n_seq = q_s.shape[0] // N_NEW
    n_pages = page_table.shape[1]
    n_pp = math.gcd(PAGES_PER_STEP, n_pages)
    n_chunks = n_pages // n_pp
    kvw = DA_KV_HEADS * 2 * DA_HEAD
    n_rows = DA_KV_HEADS * 2 * N_NEW * DA_GROUP
    q6 = q_s.reshape(n_seq, N_NEW, DA_KV_HEADS, DA_GROUP, 2, DA_HEAD)
    eye_h = jnp.eye(DA_KV_HEADS, dtype=BF16)
    eye_m = jnp.eye(2, dtype=BF16)
    qbd = jnp.einsum('btkgmd,kK,mM->bKMtgkmd', q6, eye_h, eye_m).reshape(n_seq, n_rows, kvw)
    k_new = jnp.pad(kb_s.reshape(n_seq, N_NEW, kvw).transpose(0, 2, 1), ((0, 0), (0, 0), (0, LANES - N_NEW)))
    v_new = jnp.pad(vb_s.reshape(n_seq, N_NEW, kvw), ((0, 0), (0, LANES - N_NEW), (0, 0)))
    n_pool = cache_k.shape[1]
    ck = jnp.transpose(cache_k, (0, 1, 3, 4, 5, 2)).reshape(cache_k.shape[0], n_pool, kvw, PAGE_SIZE)
    cv = cache_v.reshape(cache_v.shape[0], n_pool, PAGE_SIZE * DA_KV_HEADS, 2 * DA_HEAD)

    def page_spec(r):
        return pl.BlockSpec((None, None, kvw, PAGE_SIZE),
                            lambda b, c, pt: (layer, pt[b * n_pages + c * n_pp + r], 0, 0))

    per_seq = lambda shape: pl.BlockSpec((None,) + shape, lambda b, c, pt: (b,) + (0,) * len(shape))
    const = lambda shape: pl.BlockSpec(shape, lambda b, c, pt: (0,) * len(shape))
    kern = functools.partial(_da_sample_kernel, n_pp=n_pp, lam_init=lam_init)
    out = pl.pallas_call(
        kern,
        grid_spec=pltpu.PrefetchScalarGridSpec(
            num_scalar_prefetch=1,
            grid=(n_seq, n_chunks),
            in_specs=[const(lam_p.shape), per_seq((n_rows, kvw)), per_seq((kvw, LANES)),
                      per_seq((LANES, kvw)), const((1, LANES))]
                     + [page_spec(r) for r in range(n_pp)] + [page_spec(r) for r in range(n_pp)],
            out_specs=per_seq((DA_KV_HEADS, N_NEW * DA_GROUP, LANES)),
            scratch_shapes=[pltpu.VMEM((kvw, n_pp * PAGE_SIZE), BF16),
                            pltpu.VMEM((n_pp * PAGE_SIZE, kvw), BF16),
                            pltpu.VMEM((n_rows, LANES), F32), pltpu.VMEM((n_rows, LANES), F32),
                            pltpu.VMEM((n_rows, kvw), F32)]),
        out_shape=jax.ShapeDtypeStruct((n_seq, DA_KV_HEADS, N_NEW * DA_GROUP, LANES), F32),
        compiler_params=_params(("parallel", "arbitrary")),
        name="da_attn_sample",
    )(page_table.reshape(-1), lam_p, qbd, k_new, v_new, g_sub.reshape(1, LANES).astype(F32),
      *([ck] * n_pp), *([cv] * n_pp))
    out = out.reshape(n_seq, DA_KV_HEADS, N_NEW, DA_GROUP, LANES).transpose(0, 2, 1, 3, 4)
    return out.reshape(n_seq * N_NEW, DA_HEADS * 2 * DA_HEAD).astype(BF16)


def _post_kernel(x_ref, a_ref, wo_ref, gffn_ref, wgu_ref, wd_ref, o_ref, *, n_chunks):
    x1 = x_ref[...] + _dot(a_ref[...], wo_ref[...])
    h = _rms(x1, gffn_ref[...]).astype(BF16)
    ck = D_FF // n_chunks
    acc = x1
    for c in range(n_chunks):
        g = _dot(h, wgu_ref[:, c * ck:(c + 1) * ck])
        u = _dot(h, wgu_ref[:, D_FF + c * ck:D_FF + (c + 1) * ck])
        act = (g * jax.nn.sigmoid(g) * u).astype(BF16)
        acc = acc + _dot(act, wd_ref[c * ck:(c + 1) * ck, :])
    o_ref[...] = acc


def _post(x, a, consts):
    t = x.shape[0]
    tm = _pick_tile(t)
    ka = a.shape[1]
    kern = functools.partial(_post_kernel, n_chunks=2)
    return pl.pallas_call(
        kern,
        grid=(t // tm,),
        in_specs=[pl.BlockSpec((tm, D_MODEL), lambda i: (i, 0)),
                  pl.BlockSpec((tm, ka), lambda i: (i, 0))] + [_const_spec(c.shape) for c in consts],
        out_specs=pl.BlockSpec((tm, D_MODEL), lambda i: (i, 0)),
        out_shape=jax.ShapeDtypeStruct((t, D_MODEL), F32),
        compiler_params=_params(("parallel",)),
        name="post_ffn",
    )(x, a, *consts)


MLA_BLK = 128
ROPE_HALF = MLA_ROPE // 2


def _mla_proj_kernel(x_ref, cos_ref, sin_ref, gmix_ref, wd_ref, gqa_ref, gkva_ref, gkra_ref, gkrb_ref,
                     wuqa_ref, wuqb_ref, ga_ref, gb_ref, invn_ref, onesq_ref,
                     wuk_ref, gkn_ref, onesk_ref, wuv_ref,
                     qcat_ref, kcat_ref, v_ref, ckvf_ref, kr_ref):
    h = _rms(x_ref[...], gmix_ref[...]).astype(BF16)
    d = _dot(h, wd_ref[...])
    cq = _rms(d[:, :MLA_Q_LORA], gqa_ref[...]).astype(BF16)
    ckv = _rms(d[:, MLA_Q_LORA:MLA_Q_LORA + MLA_KV_LORA], gkva_ref[...])
    ckvf_ref[...] = ckv
    ckv_b = ckv.astype(BF16)
    cos = cos_ref[...]
    sin = sin_ref[...]
    base = MLA_Q_LORA + MLA_KV_LORA
    ka = d[:, base:base + MLA_BLK]
    kb = d[:, base + MLA_BLK:base + 2 * MLA_BLK]
    rk = lax.rsqrt(jnp.sum(ka * ka, axis=-1, keepdims=True) * (1.0 / MLA_ROPE) + EPS)
    kr = rk * (ka * gkra_ref[...] * cos + kb * gkrb_ref[...] * sin)
    kr_ref[...] = kr
    ya = _dot(cq, wuqa_ref[...])
    yb = _dot(cq, wuqb_ref[...])
    reps = MXU_DIM // MLA_BLK
    cos2 = jnp.tile(cos, (1, reps))
    sin2 = jnp.tile(sin, (1, reps))
    kr2 = jnp.tile(kr, (1, reps))
    onesq = onesq_ref[...]
    onesk = onesk_ref[...]
    kk = _dot(ckv_b, wuk_ref[...])
    for c in range(MLA_HEADS * MLA_BLK // MXU_DIM):
        sl = slice(c * MXU_DIM, (c + 1) * MXU_DIM)
        yac = ya[:, sl]
        ss = _dot((yac * yac).astype(BF16), onesq)
        r = lax.rsqrt(ss * invn_ref[:, sl] + EPS)
        qc = r * (yac * ga_ref[:, sl] * cos2 + yb[:, sl] * gb_ref[:, sl] * sin2)
        qcat_ref[:, sl] = qc.astype(BF16)
        kkc = kk[:, sl]
        ssk = _dot((kkc * kkc).astype(BF16), onesk)
        kn = kkc * lax.rsqrt(ssk * (1.0 / MLA_NOPE) + EPS) * gkn_ref[:, sl]
        kcat_ref[:, sl] = (kn + kr2).astype(BF16)
    v_ref[...] = _dot(ckv_b, wuv_ref[...]).astype(BF16)


def _head_block(rope_x1, rope_x2, nope):
    lead = (rope_x1 if rope_x1 is not None else nope).shape[:-1]
    z = lambda n: jnp.zeros(lead + (n,), F32)
    parts = [rope_x1 if rope_x1 is not None else z(ROPE_HALF),
             rope_x2 if rope_x2 is not None else z(ROPE_HALF),
             nope if nope is not None else z(MLA_NOPE),
             z(MLA_BLK - MLA_ROPE - MLA_NOPE)]
    return jnp.concatenate(parts, axis=-1)


def _rope_tables(pos):
    inv = ROPE_THETA ** (-jnp.arange(0, MLA_ROPE, 2, dtype=F32) / MLA_ROPE)
    ang = pos.astype(F32)[:, None] * inv[None, :]
    cs, sn = jnp.cos(ang), jnp.sin(ang)
    return _head_block(cs, cs, jnp.ones((pos.shape[0], MLA_NOPE), F32)), _head_block(-sn, sn, None)


def _mla_proj_consts(gmix, w_down, g_qa, g_kva, w_uq, w_uk, w_uv, g_qn, g_qr, g_kn, g_kr):
    hb = MLA_HEADS * MLA_BLK
    wd = w_down.astype(F32)
    base = MLA_Q_LORA + MLA_KV_LORA
    kx1, kx2 = wd[:, base:base + ROPE_HALF], wd[:, base + ROPE_HALF:base + MLA_ROPE]
    wd_ext = jnp.concatenate([wd[:, :base], _head_block(kx1, kx2, None), _head_block(kx2, kx1, None)],
                             axis=1).astype(BF16)
    g1, g2 = g_kr[:ROPE_HALF].astype(F32), g_kr[ROPE_HALF:].astype(F32)
    gkr_a = _head_block(g1, g2, None).reshape(1, MLA_BLK)
    gkr_b = _head_block(g2, g1, None).reshape(1, MLA_BLK)
    wq = w_uq.astype(F32).reshape(MLA_Q_LORA, MLA_HEADS, MLA_NOPE + MLA_ROPE)
    qn_w, q1_w, q2_w = wq[..., :MLA_NOPE], wq[..., MLA_NOPE:MLA_NOPE + ROPE_HALF], wq[..., MLA_NOPE + ROPE_HALF:]
    wuq_a = _head_block(q1_w, q2_w, qn_w).reshape(MLA_Q_LORA, hb).astype(BF16)
    wuq_b = _head_block(q2_w, q1_w, None).reshape(MLA_Q_LORA, hb).astype(BF16)
    r1, r2 = g_qr[:ROPE_HALF].astype(F32), g_qr[ROPE_HALF:].astype(F32)
    ga = jnp.tile(_head_block(r1, r2, g_qn.astype(F32)), MLA_HEADS).reshape(1, hb) * (MLA_SCALE * LOG2E)
    gb = jnp.tile(_head_block(r2, r1, None), MLA_HEADS).reshape(1, hb) * (MLA_SCALE * LOG2E)
    invn = jnp.tile(_head_block(jnp.full((ROPE_HALF,), 1.0 / MLA_ROPE, F32),
                                jnp.full((ROPE_HALF,), 1.0 / MLA_ROPE, F32),
                                jnp.full((MLA_NOPE,), 1.0 / MLA_NOPE, F32)), MLA_HEADS).reshape(1, hb)
    blk_groups = lambda groups: [(a + o, b + o) for o in range(0, MXU_DIM, MLA_BLK) for a, b in groups]
    ones_q = _group_ones(MXU_DIM, blk_groups([(0, MLA_ROPE), (MLA_ROPE, MLA_ROPE + MLA_NOPE)]))
    ones_k = _group_ones(MXU_DIM, blk_groups([(MLA_ROPE, MLA_ROPE + MLA_NOPE)]))
    wuk = _head_block(None, None, w_uk.astype(F32)).reshape(MLA_KV_LORA, hb).astype(BF16)
    gkn = jnp.tile(_head_block(None, None, g_kn.astype(F32)), MLA_HEADS).reshape(1, hb)
    wuv = w_uv.reshape(MLA_KV_LORA, MLA_HEADS * MLA_V).astype(BF16)
    return (gmix.reshape(1, D_MODEL).astype(F32), wd_ext,
            g_qa.reshape(1, -1).astype(F32), g_kva.reshape(1, -1).astype(F32), gkr_a, gkr_b,
            wuq_a, wuq_b, ga, gb, invn, ones_q, wuk, gkn, ones_k, wuv)


def _mla_project(x, pos, consts):
    t = x.shape[0]
    tm = _pick_tile(t)
    hb = MLA_HEADS * MLA_BLK
    cos_t, sin_t = _rope_tables(pos)
    row = lambda n: pl.BlockSpec((tm, n), lambda i: (i, 0))
    return pl.pallas_call(
        _mla_proj_kernel,
        grid=(t // tm,),
        in_specs=[row(D_MODEL), row(MLA_BLK), row(MLA_BLK)] + [_const_spec(c.shape) for c in consts],
        out_specs=[row(hb), row(hb), row(MLA_HEADS * MLA_V), row(MLA_KV_LORA), row(MLA_BLK)],
        out_shape=[jax.ShapeDtypeStruct((t, hb), BF16), jax.ShapeDtypeStruct((t, hb), BF16),
                   jax.ShapeDtypeStruct((t, MLA_HEADS * MLA_V), BF16),
                   jax.ShapeDtypeStruct((t, MLA_KV_LORA), F32),
                   jax.ShapeDtypeStruct((t, MLA_BLK), F32)],
        compiler_params=_params(("parallel",)),
        name="mla_proj",
    )(x, cos_t, sin_t, *consts)


def _mla_attn_kernel(q_ref, k_ref, v_ref, o_ref, m_sc, l_sc, acc_sc, *, tq):
    i = pl.program_id(2)
    rows = 2 * tq
    _softmax_init(m_sc, l_sc, acc_sc)

    def step(j, masked):
        off = pl.multiple_of(j * tq, tq)
        s = jnp.concatenate(
            [_dot_nt(q_ref[:, a * LANES:(a + 1) * LANES], k_ref[pl.ds(off, tq), a * LANES:(a + 1) * LANES])
             for a in range(2)], axis=0)
        if masked:
            r = lax.broadcasted_iota(jnp.int32, (rows, tq), 0) & (tq - 1)
            c = lax.broadcasted_iota(jnp.int32, (rows, tq), 1)
            s = jnp.where(c <= r, s, NEG)
        _softmax_update(s, v_ref[pl.ds(off, tq), :], m_sc, l_sc, acc_sc)

    def body(j, carry):
        step(j, False)
        return carry

    lax.fori_loop(0, i, body, 0)
    step(i, True)
    a = acc_sc[...] / l_sc[...]
    lane = lax.broadcasted_iota(jnp.int32, (tq, LANES), 1)
    o_ref[...] = jnp.where(lane < MLA_V, a[:tq], a[tq:]).astype(BF16)


def _mla_attend_prompt(qcat, kcat, v, batch, seq):
    tq = _pick_tile(seq, (512, 256, 128))
    nq = seq // tq
    kern = functools.partial(_mla_attn_kernel, tq=tq)
    return pl.pallas_call(
        kern,
        grid=(batch, MLA_HEADS // 2, nq),
        in_specs=[pl.BlockSpec((tq, 2 * MLA_BLK), lambda b, h, i: (b * nq + i, h)),
                  pl.BlockSpec((seq, 2 * MLA_BLK), lambda b, h, i: (b, h)),
                  pl.BlockSpec((seq, 2 * MLA_V), lambda b, h, i: (b, h))],
        out_specs=pl.BlockSpec((tq, 2 * MLA_V), lambda b, h, i: (b * nq + i, h)),
        out_shape=jax.ShapeDtypeStruct((batch * seq, MLA_HEADS * MLA_V), BF16),
        scratch_shapes=[pltpu.VMEM((2 * tq, LANES), F32)] * 3,
        compiler_params=_params(("parallel", "parallel", "arbitrary")),
        name="mla_attn_prompt",
    )(qcat, kcat, v)


def _mla_sample_kernel(pt_ref, qn_ref, qr_ref, cn_ref, rn_ref, wuk_ref, wukt_ref, gkn_ref, wuv_ref, *rest, n_pp):
    c_pages = rest[:n_pp]
    r_pages = rest[n_pp:2 * n_pp]
    o_ref = rest[2 * n_pp]
    lhs_sc, cbuf, rbuf, m_sc, l_sc, acc_sc = rest[2 * n_pp + 1:]
    del pt_ref
    c = pl.program_id(1)
    nw = MLA_HEADS * MLA_NOPE
    n_q = N_NEW * MLA_HEADS

    @pl.when(c == 0)
    def _():
        _softmax_init(m_sc, l_sc, acc_sc)
        qn = (qn_ref[...].astype(F32) * gkn_ref[...]).astype(BF16)
        lhs_sc[:nw, :] = wukt_ref[...]
        lhs_sc[nw:, :] = _dot_nt(qn, wuk_ref[...]).astype(BF16)

    head8 = lax.broadcasted_iota(jnp.int32, (MLA_HEADS, LANES), 0)

    def scores(cb, rt):
        out = _dot_nt(lhs_sc[...], cb)
        n = out.shape[1]
        ssq = jnp.zeros((MLA_HEADS, n), F32)
        for h in range(MLA_HEADS):
            kk = out[h * MLA_NOPE:(h + 1) * MLA_NOPE]
            ss = jnp.sum(kk * kk, axis=0, keepdims=True)
            ssq = jnp.where(jnp.tile(head8, (1, n // LANES)) == h, ss, ssq)
        rinv = lax.rsqrt(ssq * (1.0 / MLA_NOPE) + EPS)
        return out[nw:] * jnp.tile(rinv, (N_NEW, 1)) + _dot(qr_ref[...], rt)

    for r in range(n_pp):
        cbuf[r * PAGE_SIZE:(r + 1) * PAGE_SIZE, :] = c_pages[r][...].astype(BF16)
        rbuf[:, r * PAGE_SIZE:(r + 1) * PAGE_SIZE] = r_pages[r][...].astype(BF16)
    _softmax_update(scores(cbuf[...], rbuf[...]), cbuf[...], m_sc, l_sc, acc_sc)

    @pl.when(c == pl.num_programs(1) - 1)
    def _():
        row = lax.broadcasted_iota(jnp.int32, (n_q, LANES), 0)
        col = lax.broadcasted_iota(jnp.int32, (n_q, LANES), 1)
        cn = cn_ref[...].astype(BF16)
        s = jnp.where(col <= row // MLA_HEADS, scores(cn, rn_ref[...].astype(BF16)), NEG)
        _softmax_update(s, cn, m_sc, l_sc, acc_sc)
        inv = jnp.tile(1.0 / l_sc[...], (1, MLA_KV_LORA // LANES))
        o_lat = (acc_sc[...] * inv).astype(BF16)
        o2 = _dot(o_lat, wuv_ref[...])
        hrow = lax.broadcasted_iota(jnp.int32, (MLA_HEADS, MLA_HEADS * MLA_V), 0)
        hcol = lax.broadcasted_iota(jnp.int32, (MLA_HEADS, MLA_HEADS * MLA_V), 1) // MLA_V
        keep = hrow == hcol
        outs = [jnp.sum(jnp.where(keep, o2[t * MLA_HEADS:(t + 1) * MLA_HEADS, :], 0.0), axis=0, keepdims=True)
                for t in range(N_NEW)]
        o_ref[...] = jnp.concatenate(outs, axis=0)


def _mla_attend_sample(qcat_s, ckv_s, kr_s, cache_ckv, cache_kr, page_table, w_uk, g_kn, w_uv, layer):
    n_seq = qcat_s.shape[0] // N_NEW
    n_pages = page_table.shape[1]
    n_pp = math.gcd(PAGES_PER_STEP, n_pages)
    n_chunks = n_pages // n_pp
    nw = MLA_HEADS * MLA_NOPE
    n_q = N_NEW * MLA_HEADS
    q4 = qcat_s.reshape(n_seq, N_NEW, MLA_HEADS, MLA_BLK)
    q_rope = q4[..., :MLA_ROPE].reshape(n_seq, n_q, MLA_ROPE)
    q_nope = q4[..., MLA_ROPE:MLA_ROPE + MLA_NOPE]
    eye_h = jnp.eye(MLA_HEADS, dtype=BF16)
    qn_bd = jnp.einsum('bthd,hH->bthHd', q_nope, eye_h).reshape(n_seq, n_q, nw)
    c_new = jnp.pad(ckv_s.reshape(n_seq, N_NEW, MLA_KV_LORA), ((0, 0), (0, LANES - N_NEW), (0, 0)))
    r_new = jnp.pad(kr_s.reshape(n_seq, N_NEW, MLA_ROPE).transpose(0, 2, 1), ((0, 0), (0, 0), (0, LANES - N_NEW)))
    wuk = w_uk.reshape(MLA_KV_LORA, nw).astype(BF16)
    cache_kr_t = jnp.swapaxes(cache_kr, 2, 3)

    def page_spec(r, shape):
        return pl.BlockSpec((None, None) + shape,
                            lambda b, c, pt: (layer, pt[b * n_pages + c * n_pp + r], 0, 0))

    per_seq = lambda shape: pl.BlockSpec((None,) + shape, lambda b, c, pt: (b,) + (0,) * len(shape))
    const = lambda shape: pl.BlockSpec(shape, lambda b, c, pt: (0,) * len(shape))
    kern = functools.partial(_mla_sample_kernel, n_pp=n_pp)
    out = pl.pallas_call(
        kern,
        grid_spec=pltpu.PrefetchScalarGridSpec(
            num_scalar_prefetch=1,
            grid=(n_seq, n_chunks),
            in_specs=[per_seq((n_q, nw)), per_seq((n_q, MLA_ROPE)), per_seq((LANES, MLA_KV_LORA)),
                      per_seq((MLA_ROPE, LANES)), const((MLA_KV_LORA, nw)), const((nw, MLA_KV_LORA)),
                      const((1, nw)), const((MLA_KV_LORA, MLA_HEADS * MLA_V))]
                     + [page_spec(r, (PAGE_SIZE, MLA_KV_LORA)) for r in range(n_pp)]
                     + [page_spec(r, (MLA_ROPE, PAGE_SIZE)) for r in range(n_pp)],
            out_specs=per_seq((N_NEW, MLA_HEADS * MLA_V)),
            scratch_shapes=[pltpu.VMEM((nw + n_q, MLA_KV_LORA), BF16),
                            pltpu.VMEM((n_pp * PAGE_SIZE, MLA_KV_LORA), BF16),
                            pltpu.VMEM((MLA_ROPE, n_pp * PAGE_SIZE), BF16),
                            pltpu.VMEM((n_q, LANES), F32), pltpu.VMEM((n_q, LANES), F32),
                            pltpu.VMEM((n_q, MLA_KV_LORA), F32)]),
        out_shape=jax.ShapeDtypeStruct((n_seq, N_NEW, MLA_HEADS * MLA_V), F32),
        compiler_params=_params(("parallel", "arbitrary")),
        name="mla_attn_sample",
    )(page_table.reshape(-1), qn_bd, q_rope, c_new, r_new, wuk, wuk.T,
      jnp.tile(g_kn.astype(F32), MLA_HEADS).reshape(1, nw),
      w_uv.reshape(MLA_KV_LORA, MLA_HEADS * MLA_V).astype(BF16),
      *([cache_ckv] * n_pp), *([cache_kr_t] * n_pp))
    return out.reshape(n_seq * N_NEW, MLA_HEADS * MLA_V).astype(BF16)


def _hg_proj_kernel(x_ref, gmix_ref, w_ref, lbl_ref, q_ref, k_ref, lf_ref, v_ref, g_ref, *, layer):
    h = _rms(x_ref[...], gmix_ref[...]).astype(BF16)
    z = _dot(h, w_ref[...])
    lg = lbl_ref[...]
    e = jnp.exp(lg - jnp.max(lg, axis=0, keepdims=True))
    sm = e / jnp.sum(e, axis=0, keepdims=True)
    lb = jnp.sum(sm[:layer + 1], axis=0, keepdims=True) - sm[0:1]
    nk = HG_HEADS * HG_DK
    zq = z[:, :nk]
    q_ref[...] = zq * jax.nn.sigmoid(zq)
    f = lb + (1.0 - lb) * jax.nn.sigmoid(z[:, nk:2 * nk])
    k_ref[...] = 1.0 - f
    lf_ref[...] = jnp.log(f)
    v_ref[...] = z[:, 2 * nk:3 * nk]
    g_ref[...] = z[:, 3 * nk:]


def _hg_project(x, consts, layer):
    t = x.shape[0]
    tm = _pick_tile(t)
    nk = HG_HEADS * HG_DK
    row = lambda n: pl.BlockSpec((tm, n), lambda i: (i, 0))
    kern = functools.partial(_hg_proj_kernel, layer=layer)
    return pl.pallas_call(
        kern,
        grid=(t // tm,),
        in_specs=[row(D_MODEL)] + [_const_spec(c.shape) for c in consts],
        out_specs=[row(nk)] * 5,
        out_shape=[jax.ShapeDtypeStruct((t, nk), F32)] * 5,
        compiler_params=_params(("parallel",)),
        name="hg_proj",
    )(x, *consts)


def _hg_out(o, g, gout):
    return (_rms(o, gout) * (g * jax.nn.sigmoid(g))).astype(BF16)


def _hg_level_tables(chunk):
    r = jnp.arange(chunk)[:, None]
    u = jnp.arange(chunk)[None, :]
    mats, masks = [], []
    z = chunk // 2
    while z >= 1:
        ref = (r // (2 * z)) * (2 * z) + z - 1
        mats.append(jnp.where(r > ref, (u > ref) & (u <= r), (u > r) & (u <= ref)))
        masks.append((r // (2 * z) == u // (2 * z)) & (r % (2 * z) >= z) & (u % (2 * z) < z))
        z //= 2
    mats.append(u <= r)
    masks.append(u == r)
    mstack = jnp.concatenate(mats, axis=0).astype(BF16)
    return jnp.concatenate([mstack, mstack], axis=1), jnp.stack(masks).astype(F32)


def _hg_chunk_kernel(q_ref, k_ref, lf_ref, v_ref, g_ref, gout_ref, mstack_ref, mask_ref,
                     o_ref, sfin_ref, s_sc, *, chunk):
    n = pl.program_id(1)

    @pl.when(n == 0)
    def _():
        s_sc[...] = jnp.zeros(s_sc.shape, F32)

    n_lev = mask_ref.shape[0] - 1
    lf = lf_ref[...]
    hi = lf.astype(BF16)
    lo = (lf - hi.astype(F32)).astype(BF16)
    dec = _dot(mstack_ref[...], jnp.concatenate([hi, lo], axis=0))

    for h in range(HG_HEADS):
        sl = slice(h * HG_DK, (h + 1) * HG_DK)
        q = q_ref[:, sl]
        k = k_ref[:, sl]
        vb = v_ref[:, sl].astype(BF16)
        a = _dot_nt(q.astype(BF16), k.astype(BF16)) * mask_ref[n_lev]
        for lev in range(n_lev):
            gfac = jnp.exp(dec[lev * chunk:(lev + 1) * chunk, sl])
            a = a + _dot_nt((q * gfac).astype(BF16), (k * gfac).astype(BF16)) * mask_ref[lev]
        cu = dec[n_lev * chunk:, sl]
        state = s_sc[h]
        o = _dot(a.astype(BF16), vb) + _dot((q * jnp.exp(cu)).astype(BF16), state.astype(BF16))
        o_ref[:, sl] = _hg_out(o, g_ref[:, sl], gout_ref[...])
        last = cu[chunk - 1:chunk, :]
        k_dec = (k * jnp.exp(last - cu)).T.astype(BF16)
        s_sc[h] = state * _col_replicated(jnp.exp(last)) + _dot(k_dec, vb)

    @pl.when(n == pl.num_programs(1) - 1)
    def _():
        sfin_ref[...] = s_sc[...]


def _hg_recur_prompt(q, k, lf, v, g, g_out, batch, seq):
    chunk = HG_CHUNK
    nc = seq // chunk
    nk = HG_HEADS * HG_DK
    mstack, masks = _hg_level_tables(chunk)
    blk = pl.BlockSpec((chunk, nk), lambda b, n: (b * nc + n, 0))
    kern = functools.partial(_hg_chunk_kernel, chunk=chunk)
    return pl.pallas_call(
        kern,
        grid=(batch, nc),
        in_specs=[blk] * 5 + [_const_spec((1, HG_DV)), _const_spec(mstack.shape), _const_spec(masks.shape)],
        out_specs=[blk, pl.BlockSpec((None, HG_HEADS, HG_DK, HG_DV), lambda b, n: (b, 0, 0, 0))],
        out_shape=[jax.ShapeDtypeStruct((batch * seq, nk), BF16),
                   jax.ShapeDtypeStruct((batch, HG_HEADS, HG_DK, HG_DV), F32)],
        scratch_shapes=[pltpu.VMEM((HG_HEADS, HG_DK, HG_DV), F32)],
        compiler_params=_params(("parallel", "arbitrary")),
        name="hg_recur_prompt",
    )(q, k, lf, v, g, g_out.reshape(1, HG_DV).astype(F32), mstack, masks)


def _hg_sample_kernel(q_ref, k_ref, lf_ref, v_ref, g_ref, gout_ref, s0_ref, o_ref, s_ref, *, n_blk):
    row8 = lax.broadcasted_iota(jnp.int32, (8, LANES), 0)

    def rows_to_tile(rows):
        top = jnp.zeros((8, LANES), F32)
        for t, r in enumerate(rows):
            top = jnp.where(row8 == t, r, top)
        return jnp.concatenate([top, jnp.zeros((LANES - 8, LANES), F32)], axis=0)

    def per_seq(b, carry):
        for h in range(HG_HEADS):
            sl = slice(h * HG_DK, (h + 1) * HG_DK)
            q = [q_ref[b, t:t + 1, sl] for t in range(N_NEW)]
            k = [k_ref[b, t:t + 1, sl] for t in range(N_NEW)]
            v = [v_ref[b, t:t + 1, sl] for t in range(N_NEW)]
            cum = []
            for t in range(N_NEW):
                lf_t = lf_ref[b, t:t + 1, sl]
                cum.append(lf_t if t == 0 else cum[-1] + lf_t)
            s0 = s0_ref[b, h]
            q_dec = rows_to_tile([q[t] * jnp.exp(cum[t]) for t in range(N_NEW)])[:16]
            inter = _dot(q_dec.astype(BF16), s0.astype(BF16))
            for t in range(N_NEW):
                o = inter[t:t + 1]
                for s in range(t + 1):
                    a = jnp.sum(q[t] * k[s] * jnp.exp(cum[t] - cum[s]), axis=-1, keepdims=True)
                    o = o + a * v[s]
                o_ref[b, t:t + 1, sl] = _rms(o, gout_ref[...]) * (
                    g_ref[b, t:t + 1, sl] * jax.nn.sigmoid(g_ref[b, t:t + 1, sl]))
            last = cum[-1]
            k_dec = rows_to_tile([k[s] * jnp.exp(last - cum[s]) for s in range(N_NEW)])
            v_tile = rows_to_tile(v)
            s_ref[b, h] = (s0 * _col_replicated(jnp.exp(last))
                           + _dot(k_dec.T.astype(BF16), v_tile.astype(BF16)))
        return carry

    lax.fori_loop(0, n_blk, per_seq, 0)


def _hg_recur_sample(q, k, lf, v, g, g_out, state):
    n_seq, _, nk = q.shape
    n_blk = _pick_tile(n_seq, (HG_SEQ_BLOCK, 4, 2, 1))
    tok = pl.BlockSpec((n_blk, N_NEW, nk), lambda i: (i, 0, 0))
    st = pl.BlockSpec((n_blk, HG_HEADS, HG_DK, HG_DV), lambda i: (i, 0, 0, 0))
    kern = functools.partial(_hg_sample_kernel, n_blk=n_blk)
    return pl.pallas_call(
        kern,
        grid=(n_seq // n_blk,),
        in_specs=[tok] * 5 + [pl.BlockSpec((1, HG_DV), lambda i: (0, 0)), st],
        out_specs=[tok, st],
        out_shape=[jax.ShapeDtypeStruct((n_seq, N_NEW, nk), F32),
                   jax.ShapeDtypeStruct(state.shape, F32)],
        compiler_params=_params(("parallel",)),
        name="hg_recur_sample",
    )(q, k, lf, v, g, g_out.reshape(1, HG_DV).astype(F32), state.astype(F32))


def kernel(x_prompt, x_sample, cache_da_k, cache_da_v, cache_mla_ckv, cache_mla_krope, state_hgrn, page_table, norm_mix, norm_ffn, ffn_w_gu, ffn_w_down, da_w_qkv, da_q_gain, da_k_gain, da_lambda, da_sub_gain, da_w_o, mla_w_down, mla_qa_gain, mla_kva_gain, mla_w_uq, mla_w_uk, mla_w_uv, mla_qn_gain, mla_qr_gain, mla_kn_gain, mla_kr_gain, mla_w_o, hg_w_in, hg_lb_logits, hg_out_gain, hg_w_o):
    batch, seq, _ = x_prompt.shape
    n_seq, n_new, _ = x_sample.shape
    assert n_new == N_NEW and seq % HG_CHUNK == 0
    tp, ts = batch * seq, n_seq * n_new
    past = page_table.shape[1] * PAGE_SIZE
    depth = norm_mix.shape[0]
    xp = x_prompt.reshape(tp, D_MODEL).astype(F32)
    xs = x_sample.reshape(ts, D_MODEL).astype(F32)
    pos_p = jnp.tile(jnp.arange(seq), batch)
    pos_s = jnp.tile(past + jnp.arange(n_new), n_seq)

    da_kp, da_vp, da_ks, da_vs = [], [], [], []
    mla_cp, mla_rp, mla_cs, mla_rs = [], [], [], []
    hg_p, hg_s = [], []
    for i in range(depth):
        kind, j = i % N_MIXERS, i // N_MIXERS
        if kind == 0:
            lam_init = 0.8 - 0.6 * math.exp(-0.3 * i)
            lam_p = da_lambda[j].astype(F32)
            consts = _da_proj_consts(norm_mix[i], da_w_qkv[j], da_q_gain[j], da_k_gain[j])
            q_p, kf_p, vf_p, kb_p, vb_p = _da_project(xp, consts, seq)
            q_s, kf_s, vf_s, kb_s, vb_s = _da_project(xs, consts)
            a_p = _da_attend_prompt(q_p, kb_p, vb_p, lam_p, da_sub_gain[j], lam_init, batch, seq)
            a_s = _da_attend_sample(q_s, kb_s, vb_s, cache_da_k, cache_da_v, page_table,
                                    lam_p, da_sub_gain[j], lam_init, j)
            w_o = da_w_o[j]
            da_kp.append(kf_p.reshape(batch, DA_KV_HEADS, 2, DA_HEAD, seq).transpose(0, 4, 1, 2, 3))
            da_vp.append(vf_p.reshape(batch, seq, DA_KV_HEADS, 2 * DA_HEAD))
            da_ks.append(kf_s.reshape(n_seq, n_new, DA_KV_HEADS, 2, DA_HEAD))
            da_vs.append(vf_s.reshape(n_seq, n_new, DA_KV_HEADS, 2 * DA_HEAD))
        elif kind == 1:
            consts = _mla_proj_consts(norm_mix[i], mla_w_down[j], mla_qa_gain[j], mla_kva_gain[j], mla_w_uq[j],
                                      mla_w_uk[j], mla_w_uv[j], mla_qn_gain[j], mla_qr_gain[j],
                                      mla_kn_gain[j], mla_kr_gain[j])
            qcat_p, kcat_p, v_p, ckv_p, kr_p = _mla_project(xp, pos_p, consts)
            qcat_s, _, _, ckv_s, kr_s = _mla_project(xs, pos_s, consts)
            kr_p, kr_s = kr_p[:, :MLA_ROPE], kr_s[:, :MLA_ROPE]
            a_p = _mla_attend_prompt(qcat_p, kcat_p, v_p, batch, seq)
            a_s = _mla_attend_sample(qcat_s, ckv_s, kr_s, cache_mla_ckv, cache_mla_krope,
                                     page_table, mla_w_uk[j], mla_kn_gain[j], mla_w_uv[j], j)
            w_o = mla_w_o[j]
            mla_cp.append(ckv_p.reshape(batch, seq, MLA_KV_LORA))
            mla_rp.append(kr_p.reshape(batch, seq, MLA_ROPE))
            mla_cs.append(ckv_s.reshape(n_seq, n_new, MLA_KV_LORA))
            mla_rs.append(kr_s.reshape(n_seq, n_new, MLA_ROPE))
        else:
            consts = (norm_mix[i].reshape(1, D_MODEL).astype(F32), hg_w_in[j].astype(BF16),
                      hg_lb_logits.astype(F32))
            hq, hk, hlf, hv, hgate = _hg_project(xp, consts, i)
            a_p, s_p = _hg_recur_prompt(hq, hk, hlf, hv, hgate, hg_out_gain[j], batch, seq)
            smp = lambda a: a.reshape(n_seq, n_new, HG_HEADS * HG_DK)
            a_s, s_s = _hg_recur_sample(*[smp(a) for a in _hg_project(xs, consts, i)],
                                        hg_out_gain[j], state_hgrn[j])
            a_s = a_s.reshape(ts, HG_HEADS * HG_DV).astype(BF16)
            w_o = hg_w_o[j]
            hg_p.append(s_p.astype(state_hgrn.dtype))
            hg_s.append(s_s.astype(state_hgrn.dtype))
        post_consts = (w_o.astype(BF16), norm_ffn[i].reshape(1, D_MODEL).astype(F32),
                       ffn_w_gu[i].astype(BF16), ffn_w_down[i].astype(BF16))
        xp = _post(xp, a_p, post_consts)
        xs = _post(xs, a_s, post_consts)

    return (xp.reshape(batch, seq, D_MODEL), xs.reshape(n_seq, n_new, D_MODEL),
            jnp.stack(da_kp), jnp.stack(da_vp), jnp.stack(da_ks), jnp.stack(da_vs),
            jnp.stack(mla_cp), jnp.stack(mla_rp), jnp.stack(mla_cs), jnp.stack(mla_rs),
            jnp.stack(hg_p), jnp.stack(hg_s))
```

```python
import functools
import math

import jax
import jax.numpy as jnp
from jax import lax
from jax.experimental import pallas as pl
from jax.experimental.pallas import tpu as pltpu

F32 = jnp.float32
BF16 = jnp.bfloat16

D_MODEL = 1024
PAGE_SIZE = 128
N_MIXERS = 3
N_NEW = 4
DA_HEADS = 8
DA_KV_HEADS = 4
DA_GROUP = DA_HEADS // DA_KV_HEADS
DA_HEAD = 64
MLA_HEADS = 8
MLA_NOPE = 64
MLA_ROPE = 32
MLA_V = 64
MLA_Q_LORA = 256
MLA_KV_LORA = 256
MLA_SCALE = (MLA_NOPE + MLA_ROPE) ** -0.5
ROPE_THETA = 10000.0
HG_HEADS = 8
HG_DK = 128
HG_DV = 128
D_FF = 2816
EPS = 1e-6

LOG2E = math.log2(math.e)
NEG = -1e30
LANES = 128
MXU_DIM = 256
VMEM_LIMIT = 56 * 1024 * 1024
PAGES_PER_STEP = 16
HG_CHUNK = 128
HG_SEQ_BLOCK = 8


def _pick_tile(n, candidates=(512, 256, 128, 64, 32, 16, 8)):
    for c in candidates:
        if n % c == 0:
            return c
    raise ValueError(f"no tile divides {n}")


def _const_spec(shape):
    nd = len(shape)
    return pl.BlockSpec(shape, lambda *_: (0,) * nd, pipeline_mode=pl.Buffered(1))


def _params(sem):
    return pltpu.CompilerParams(dimension_semantics=sem, vmem_limit_bytes=VMEM_LIMIT)


def _rms(x, g):
    ms = jnp.mean(x * x, axis=-1, keepdims=True)
    return x * lax.rsqrt(ms + EPS) * g


def _dot(a, b):
    return jnp.dot(a, b, preferred_element_type=F32)


def _dot_nt(a, b):
    return lax.dot_general(a, b, (((1,), (1,)), ((), ())), preferred_element_type=F32)


def _group_ones(width, groups):
    idx = jnp.arange(width)
    gid = jnp.full((width,), -1, jnp.int32)
    for n, (a, b) in enumerate(groups):
        gid = jnp.where((idx >= a) & (idx < b), n, gid)
    m = (gid[:, None] == gid[None, :]) & (gid[:, None] >= 0)
    return m.astype(BF16)


def _lambda_value(lv, lam_init):
    a = jnp.sum(lv[0:1] * lv[1:2], axis=-1, keepdims=True)
    b = jnp.sum(lv[2:3] * lv[3:4], axis=-1, keepdims=True)
    return jnp.exp(a) - jnp.exp(b) + lam_init


def _col_replicated(row):
    return jnp.broadcast_to(row, (LANES, LANES)).T


def _softmax_init(m_sc, l_sc, acc_sc):
    m_sc[...] = jnp.full(m_sc.shape, NEG, F32)
    l_sc[...] = jnp.zeros(l_sc.shape, F32)
    acc_sc[...] = jnp.zeros(acc_sc.shape, F32)


def _softmax_update(s, vb, m_sc, l_sc, acc_sc):
    n = s.shape[1]
    m_prev = m_sc[...]
    m_new = jnp.maximum(m_prev, jnp.max(s, axis=-1, keepdims=True))
    alpha = jnp.exp2(m_prev - m_new)
    p = jnp.exp2(s - jnp.tile(m_new, (1, n // LANES)))
    l_sc[...] = alpha * l_sc[...] + jnp.sum(p, axis=-1, keepdims=True)
    m_sc[...] = m_new
    acc_sc[...] = (jnp.tile(alpha, (1, acc_sc.shape[1] // LANES)) * acc_sc[...]
                   + _dot(p.astype(BF16), vb))


def _da_proj_kernel(x_ref, gmix_ref, w_ref, gq_ref, gk_ref, ones_ref,
                    q_ref, kf_ref, vf_ref, kb_ref, vb_ref, *, cache_layout):
    tm = x_ref.shape[0]
    h = _rms(x_ref[...], gmix_ref[...]).astype(BF16)
    y = _dot(h, w_ref[...])
    ones = ones_ref[...]

    def group_norm(yc, g):
        ss = _dot((yc * yc).astype(BF16), ones)
        return yc * lax.rsqrt(ss * (1.0 / DA_HEAD) + EPS) * g

    nq = DA_HEADS * 2 * DA_HEAD
    nk = DA_KV_HEADS * 2 * DA_HEAD
    for c in range(nq // MXU_DIM):
        sl = slice(c * MXU_DIM, (c + 1) * MXU_DIM)
        q_ref[:, sl] = group_norm(y[:, sl], gq_ref[:, sl]).astype(BF16)
    for c in range(nk // MXU_DIM):
        sl = slice(c * MXU_DIM, (c + 1) * MXU_DIM)
        kn = group_norm(y[:, nq + c * MXU_DIM:nq + (c + 1) * MXU_DIM], gk_ref[:, sl])
        if cache_layout:
            kf_ref[sl, :] = kn.T
        else:
            kf_ref[:, sl] = kn
        kb_ref[:, sl] = kn.astype(BF16)
    v = y[:, nq + nk:]
    if cache_layout:
        for hd in range(DA_KV_HEADS):
            vf_ref[pl.ds(hd, tm, stride=DA_KV_HEADS), :] = v[:, hd * LANES:(hd + 1) * LANES]
    else:
        vf_ref[...] = v
    vb_ref[...] = v.astype(BF16)


def _da_proj_consts(gmix, w_qkv, g_q, g_k):
    nq = DA_HEADS * 2 * DA_HEAD
    nk = DA_KV_HEADS * 2 * DA_HEAD
    gq_row = (jnp.tile(g_q.astype(F32), nq // DA_HEAD) * (DA_HEAD ** -0.5 * LOG2E)).reshape(1, nq)
    gk_row = jnp.tile(g_k.astype(F32), nk // DA_HEAD).reshape(1, nk)
    ones = _group_ones(MXU_DIM, [(i * DA_HEAD, (i + 1) * DA_HEAD) for i in range(MXU_DIM // DA_HEAD)])
    return (gmix.reshape(1, D_MODEL).astype(F32), w_qkv.astype(BF16), gq_row, gk_row, ones)


def _da_project(x, consts, seq=None):
    t = x.shape[0]
    tm = _pick_tile(t if seq is None else seq)
    nq = DA_HEADS * 2 * DA_HEAD
    nk = DA_KV_HEADS * 2 * DA_HEAD
    row = lambda n: pl.BlockSpec((tm, n), lambda i: (i, 0))
    if seq is None:
        kv_specs = [row(nk), row(nk)]
        kv_shapes = [jax.ShapeDtypeStruct((t, nk), F32), jax.ShapeDtypeStruct((t, nk), F32)]
    else:
        per = seq // tm
        kv_specs = [pl.BlockSpec((None, nk, tm), lambda i: (i // per, 0, i % per)),
                    pl.BlockSpec((DA_KV_HEADS * tm, 2 * DA_HEAD), lambda i: (i, 0))]
        kv_shapes = [jax.ShapeDtypeStruct((t // seq, nk, seq), F32),
                     jax.ShapeDtypeStruct((t * DA_KV_HEADS, 2 * DA_HEAD), F32)]
    kern = functools.partial(_da_proj_kernel, cache_layout=seq is not None)
    return pl.pallas_call(
        kern,
        grid=(t // tm,),
        in_specs=[row(D_MODEL)] + [_const_spec(c.shape) for c in consts],
        out_specs=[row(nq)] + kv_specs + [row(nk), row(nk)],
        out_shape=[jax.ShapeDtypeStruct((t, nq), BF16)] + kv_shapes
                  + [jax.ShapeDtypeStruct((t, nk), BF16), jax.ShapeDtypeStruct((t, nk), BF16)],
        compiler_params=_params(("parallel",)),
        name="da_proj",
    )(x, *consts)


def _da_attn_kernel(lam_ref, q_ref, k_ref, v_ref, gsub_ref, o_ref,
                    qs_sc, m_sc, l_sc, acc_sc, *, tq, lam_init):
    i = pl.program_id(2)
    rows = 2 * DA_GROUP * tq
    q = q_ref[...]
    lane = lax.broadcasted_iota(jnp.int32, (tq, LANES), 1)
    first_map = lane < DA_HEAD
    zero = jnp.zeros((tq, LANES), BF16)
    for g in range(DA_GROUP):
        qg = q[:, g * LANES:(g + 1) * LANES]
        qs_sc[(2 * g) * tq:(2 * g + 1) * tq, :] = jnp.where(first_map, qg, zero)
        qs_sc[(2 * g + 1) * tq:(2 * g + 2) * tq, :] = jnp.where(first_map, zero, qg)
    _softmax_init(m_sc, l_sc, acc_sc)

    def step(j, masked):
        off = pl.multiple_of(j * tq, tq)
        s = _dot_nt(qs_sc[...], k_ref[pl.ds(off, tq), :])
        if masked:
            r = lax.broadcasted_iota(jnp.int32, (rows, tq), 0) & (tq - 1)
            c = lax.broadcasted_iota(jnp.int32, (rows, tq), 1)
            s = jnp.where(c <= r, s, NEG)
        _softmax_update(s, v_ref[pl.ds(off, tq), :], m_sc, l_sc, acc_sc)

    def body(j, carry):
        step(j, False)
        return carry

    lax.fori_loop(0, i, body, 0)
    step(i, True)

    lam = _lambda_value(lam_ref[...], lam_init)
    a = acc_sc[...] / l_sc[...]
    for g in range(DA_GROUP):
        o = a[(2 * g) * tq:(2 * g + 1) * tq] - lam * a[(2 * g + 1) * tq:(2 * g + 2) * tq]
        on = _rms(o, gsub_ref[...]) * (1.0 - lam_init)
        o_ref[:, g * LANES:(g + 1) * LANES] = on.astype(BF16)


def _da_attend_prompt(q, kb, vb, lam_p, g_sub, lam_init, batch, seq):
    tq = _pick_tile(seq, (512, 256, 128))
    nq = seq // tq
    rows = 2 * DA_GROUP * tq
    kern = functools.partial(_da_attn_kernel, tq=tq, lam_init=lam_init)
    return pl.pallas_call(
        kern,
        grid=(batch, DA_KV_HEADS, nq),
        in_specs=[_const_spec(lam_p.shape),
                  pl.BlockSpec((tq, 2 * LANES), lambda b, h, i: (b * nq + i, h)),
                  pl.BlockSpec((seq, LANES), lambda b, h, i: (b, h)),
                  pl.BlockSpec((seq, LANES), lambda b, h, i: (b, h)),
                  _const_spec((1, LANES))],
        out_specs=pl.BlockSpec((tq, 2 * LANES), lambda b, h, i: (b * nq + i, h)),
        out_shape=jax.ShapeDtypeStruct((batch * seq, DA_HEADS * 2 * DA_HEAD), BF16),
        scratch_shapes=[pltpu.VMEM((rows, LANES), BF16), pltpu.VMEM((rows, LANES), F32),
                        pltpu.VMEM((rows, LANES), F32), pltpu.VMEM((rows, LANES), F32)],
        compiler_params=_params(("parallel", "parallel", "arbitrary")),
        name="da_attn_prompt",
    )(lam_p, q, kb, vb, g_sub.reshape(1, LANES).astype(F32))


def _da_sample_chunk(c, n_chunks, lam_ref, qbd_ref, kn_ref, vn_ref, gsub_ref, k_pages, v_pages, o_ref,
                     kbuf, vbuf, m_sc, l_sc, acc_sc, lam_init):
    n_pp = len(k_pages)

    @pl.when(c == 0)
    def _():
        _softmax_init(m_sc, l_sc, acc_sc)

    qbd = qbd_ref[...]
    for r in range(n_pp):
        cols = slice(r * PAGE_SIZE, (r + 1) * PAGE_SIZE)
        kbuf[:, cols] = k_pages[r][...].astype(BF16)
        for h in range(DA_KV_HEADS):
            vbuf[cols, h * LANES:(h + 1) * LANES] = (
                v_pages[r][pl.ds(h, PAGE_SIZE, stride=DA_KV_HEADS), :].astype(BF16))
    _softmax_update(_dot(qbd, kbuf[...]), vbuf[...], m_sc, l_sc, acc_sc)

    @pl.when(c == n_chunks - 1)
    def _():
        n_rows = qbd.shape[0]
        row = lax.broadcasted_iota(jnp.int32, (n_rows, LANES), 0)
        col = lax.broadcasted_iota(jnp.int32, (n_rows, LANES), 1)
        t_row = (row >> 1) & (N_NEW - 1)
        s = jnp.where(col <= t_row, _dot(qbd, kn_ref[...]), NEG)
        _softmax_update(s, vn_ref[...], m_sc, l_sc, acc_sc)

        lam = _lambda_value(lam_ref[...], lam_init)
        inv = 1.0 / l_sc[...]
        acc = acc_sc[...]
        half = N_NEW * DA_GROUP
        for h in range(DA_KV_HEADS):
            rows = slice(h * 2 * half, (h + 1) * 2 * half)
            blk = acc[rows, h * LANES:(h + 1) * LANES] * inv[rows, :]
            o = blk[:half] - lam * blk[half:]
            o_ref[h] = _rms(o, gsub_ref[...]) * (1.0 - lam_init)


def _da_sample_kernel(pt_ref, lam_ref, qbd_ref, kn_ref, vn_ref, gsub_ref, *rest, n_pp, lam_init):
    del pt_ref
    _da_sample_chunk(pl.program_id(1), pl.num_programs(1), lam_ref, qbd_ref, kn_ref, vn_ref, gsub_ref,
                     rest[:n_pp], rest[n_pp:2 * n_pp], rest[2 * n_pp], *rest[2 * n_pp + 1:], lam_init)


def _da_fused_kernel(pt_ref, lam_ref, q_ref, k_ref, v_ref, gsub_ref, qbd_ref, kn_ref, vn_ref, *rest,
                     n_pp, tq, lam_init):
    k_pages = rest[:n_pp]
    v_pages = rest[n_pp:2 * n_pp]
    o_ref, os_ref = rest[2 * n_pp:2 * n_pp + 2]
    qs_sc, m_sc, l_sc, acc_sc, kbuf, vbuf, ms_sc, ls_sc, accs_sc = rest[2 * n_pp + 2:]
    del pt_ref
    i = pl.program_id(2)
    s = pl.program_id(3)
    n_tiles = 2 * DA_GROUP

    lane = lax.broadcasted_iota(jnp.int32, (tq, LANES), 1)
    first = (s % 2) * DA_HEAD
    keep = (lane >= first) & (lane < first + DA_HEAD)
    zero = jnp.zeros((tq, LANES), BF16)
    for g in range(DA_GROUP):
        @pl.when(s // 2 == g)
        def _(g=g):
            qs_sc[...] = jnp.where(keep, q_ref[:, g * LANES:(g + 1) * LANES], zero)

    rows = pl.ds(pl.multiple_of(s * tq, tq), tq)
    m_t, l_t, acc_t = m_sc.at[rows], l_sc.at[rows], acc_sc.at[rows]
    _softmax_init(m_t, l_t, acc_t)

    def step(j, masked):
        off = pl.multiple_of(j * tq, tq)
        sc = _dot_nt(qs_sc[...], k_ref[pl.ds(off, tq), :])
        if masked:
            r = lax.broadcasted_iota(jnp.int32, (tq, tq), 0)
            c = lax.broadcasted_iota(jnp.int32, (tq, tq), 1)
            sc = jnp.where(c <= r, sc, NEG)
        _softmax_update(sc, v_ref[pl.ds(off, tq), :], m_t, l_t, acc_t)

    def body(jj, carry):
        step(2 * jj, False)
        step(2 * jj + 1, False)
        return carry

    lax.fori_loop(0, i // 2, body, 0)

    @pl.when(i % 2 == 1)
    def _():
        step(i - 1, False)

    step(i, True)

    _da_sample_chunk(s, n_tiles, lam_ref, qbd_ref, kn_ref, vn_ref, gsub_ref, k_pages, v_pages, os_ref,
                     kbuf, vbuf, ms_sc, ls_sc, accs_sc, lam_init)

    @pl.when(s == n_tiles - 1)
    def _():
        lam = _lambda_value(lam_ref[...], lam_init)
        a = acc_sc[...] / l_sc[...]
        for g in range(DA_GROUP):
            o = a[(2 * g) * tq:(2 * g + 1) * tq] - lam * a[(2 * g + 1) * tq:(2 * g + 2) * tq]
            on = _rms(o, gsub_ref[...]) * (1.0 - lam_init)
            o_ref[:, g * LANES:(g + 1) * LANES] = on.astype(BF16)


def _da_sample_operands(q_s, kb_s, vb_s, cache_k, cache_v):
    n_seq = q_s.shape[0] // N_NEW
    kvw = DA_KV_HEADS * 2 * DA_HEAD
    n_rows = DA_KV_HEADS * 2 * N_NEW * DA_GROUP
    q6 = q_s.reshape(n_seq, N_NEW, DA_KV_HEADS, DA_GROUP, 2, DA_HEAD)
    eye_h = jnp.eye(DA_KV_HEADS, dtype=BF16)
    eye_m = jnp.eye(2, dtype=BF16)
    qbd = jnp.einsum('btkgmd,kK,mM->bKMtgkmd', q6, eye_h, eye_m).reshape(n_seq, n_rows, kvw)
    k_new = jnp.pad(kb_s.reshape(n_seq, N_NEW, kvw).transpose(0, 2, 1), ((0, 0), (0, 0), (0, LANES - N_NEW)))
    v_new = jnp.pad(vb_s.reshape(n_seq, N_NEW, kvw), ((0, 0), (0, LANES - N_NEW), (0, 0)))
    n_pool = cache_k.shape[1]
    ck = jnp.transpose(cache_k, (0, 1, 3, 4, 5, 2)).reshape(cache_k.shape[0], n_pool, kvw, PAGE_SIZE)
    cv = cache_v.reshape(cache_v.shape[0], n_pool, PAGE_SIZE * DA_KV_HEADS, 2 * DA_HEAD)
    return qbd, k_new, v_new, ck, cv


def _da_sample_output(out):
    n_seq = out.shape[0]
    out = out.reshape(n_seq, DA_KV_HEADS, N_NEW, DA_GROUP, LANES).transpose(0, 2, 1, 3, 4)
    return out.reshape(n_seq * N_NEW, DA_HEADS * 2 * DA_HEAD).astype(BF16)


def _da_attend_fused(q_p, kb_p, vb_p, q_s, kb_s, vb_s, cache_k, cache_v, page_table, lam_p, g_sub,
                     lam_init, layer, batch, seq):
    tq = _pick_tile(seq, (512, 256, 128))
    nq = seq // tq
    n_tiles = 2 * DA_GROUP
    n_seq = q_s.shape[0] // N_NEW
    n_pages = page_table.shape[1]
    assert n_seq == batch * DA_KV_HEADS * nq and n_pages % n_tiles == 0
    n_pp = n_pages // n_tiles
    kvw = DA_KV_HEADS * 2 * DA_HEAD
    n_rows = DA_KV_HEADS * 2 * N_NEW * DA_GROUP
    qbd, k_new, v_new, ck, cv = _da_sample_operands(q_s, kb_s, vb_s, cache_k, cache_v)
    seq_of = lambda b, h, i: (b * DA_KV_HEADS + h) * nq + i

    def page_spec(r):
        return pl.BlockSpec((None, None, kvw, PAGE_SIZE),
                            lambda b, h, i, s, pt: (layer, pt[seq_of(b, h, i) * n_pages + s * n_pp + r], 0, 0))

    per_seq = lambda shape: pl.BlockSpec((None,) + shape,
                                         lambda b, h, i, s, pt: (seq_of(b, h, i),) + (0,) * len(shape))
    const = lambda shape: pl.BlockSpec(shape, lambda b, h, i, s, pt: (0,) * len(shape))
    q_blk = pl.BlockSpec((tq, 2 * LANES), lambda b, h, i, s, pt: (b * nq + i, h))
    kv_blk = pl.BlockSpec((seq, LANES), lambda b, h, i, s, pt: (b, h))
    rows = n_tiles * tq
    kern = functools.partial(_da_fused_kernel, n_pp=n_pp, tq=tq, lam_init=lam_init)
    a_p, out_s = pl.pallas_call(
        kern,
        grid_spec=pltpu.PrefetchScalarGridSpec(
            num_scalar_prefetch=1,
            grid=(batch, DA_KV_HEADS, nq, n_tiles),
            in_specs=[const(lam_p.shape), q_blk, kv_blk, kv_blk, const((1, LANES)),
                      per_seq((n_rows, kvw)), per_seq((kvw, LANES)), per_seq((LANES, kvw))]
                     + [page_spec(r) for r in range(n_pp)] + [page_spec(r) for r in range(n_pp)],
            out_specs=[q_blk, per_seq((DA_KV_HEADS, N_NEW * DA_GROUP, LANES))],
            scratch_shapes=[pltpu.VMEM((tq, LANES), BF16), pltpu.VMEM((rows, LANES), F32),
                            pltpu.VMEM((rows, LANES), F32), pltpu.VMEM((rows, LANES), F32),
                            pltpu.VMEM((kvw, n_pp * PAGE_SIZE), BF16),
                            pltpu.VMEM((n_pp * PAGE_SIZE, kvw), BF16),
                            pltpu.VMEM((n_rows, LANES), F32), pltpu.VMEM((n_rows, LANES), F32),
                            pltpu.VMEM((n_rows, kvw), F32)]),
        out_shape=[jax.ShapeDtypeStruct((batch * seq, DA_HEADS * 2 * DA_HEAD), BF16),
                   jax.ShapeDtypeStruct((n_seq, DA_KV_HEADS, N_NEW * DA_GROUP, LANES), F32)],
        compiler_params=_params(("parallel", "parallel", "arbitrary", "arbitrary")),
        name="da_attn_fused",
    )(page_table.reshape(-1), lam_p, q_p, kb_p, vb_p, g_sub.reshape(1, LANES).astype(F32),
      qbd, k_new, v_new, *([ck] * n_pp), *([cv] * n_pp))
    return a_p, _da_sample_output(out_s)


def _da_attend_sample(q_s, kb_s, vb_s, cache_k, cache_v, page_table, lam_p, g_sub, lam_init, layer):
    n_seq = q_s.shape[0] // N_NEW
    n_pages = page_table.shape[1]
    n_pp = math.gcd(PAGES_PER_STEP, n_pages)
    n_chunks = n_pages // n_pp
    kvw = DA_KV_HEADS * 2 * DA_HEAD
    n_rows = DA_KV_HEADS * 2 * N_NEW * DA_GROUP
    qbd, k_new, v_new, ck, cv = _da_sample_operands(q_s, kb_s, vb_s, cache_k, cache_v)

    def page_spec(r):
        return pl.BlockSpec((None, None, kvw, PAGE_SIZE),
                            lambda b, c, pt: (layer, pt[b * n_pages + c * n_pp + r], 0, 0))

    per_seq = lambda shape: pl.BlockSpec((None,) + shape, lambda b, c, pt: (b,) + (0,) * len(shape))
    const = lambda shape: pl.BlockSpec(shape, lambda b, c, pt: (0,) * len(shape))
    kern = functools.partial(_da_sample_kernel, n_pp=n_pp, lam_init=lam_init)
    out = pl.pallas_call(
        kern,
        grid_spec=pltpu.PrefetchScalarGridSpec(
            num_scalar_prefetch=1,
            grid=(n_seq, n_chunks),
            in_specs=[const(lam_p.shape), per_seq((n_rows, kvw)), per_seq((kvw, LANES)),
                      per_seq((LANES, kvw)), const((1, LANES))]
                     + [page_spec(r) for r in range(n_pp)] + [page_spec(r) for r in range(n_pp)],
            out_specs=per_seq((DA_KV_HEADS, N_NEW * DA_GROUP, LANES)),
            scratch_shapes=[pltpu.VMEM((kvw, n_pp * PAGE_SIZE), BF16),
                            pltpu.VMEM((n_pp * PAGE_SIZE, kvw), BF16),
                            pltpu.VMEM((n_rows, LANES), F32), pltpu.VMEM((n_rows, LANES), F32),
                            pltpu.VMEM((n_rows, kvw), F32)]),
        out_shape=jax.ShapeDtypeStruct((n_seq, DA_KV_HEADS, N_NEW * DA_GROUP, LANES), F32),
        compiler_params=_params(("parallel", "arbitrary")),
        name="da_attn_sample",
    )(page_table.reshape(-1), lam_p, qbd, k_new, v_new, g_sub.reshape(1, LANES).astype(F32),
      *([ck] * n_pp), *([cv] * n_pp))
    return _da_sample_output(out)


def _post_kernel(x_ref, a_ref, wo_ref, gffn_ref, wgu_ref, wd_ref, o_ref, *, n_chunks):
    x1 = x_ref[...] + _dot(a_ref[...], wo_ref[...])
    h = _rms(x1, gffn_ref[...]).astype(BF16)
    ck = D_FF // n_chunks
    acc = x1
    for c in range(n_chunks):
        g = _dot(h, wgu_ref[:, c * ck:(c + 1) * ck])
        u = _dot(h, wgu_ref[:, D_FF + c * ck:D_FF + (c + 1) * ck])
        act = (g * jax.nn.sigmoid(g) * u).astype(BF16)
        acc = acc + _dot(act, wd_ref[c * ck:(c + 1) * ck, :])
    o_ref[...] = acc


def _post(x, a, consts):
    t = x.shape[0]
    tm = _pick_tile(t)
    ka = a.shape[1]
    kern = functools.partial(_post_kernel, n_chunks=2)
    return pl.pallas_call(
        kern,
        grid=(t // tm,),
        in_specs=[pl.BlockSpec((tm, D_MODEL), lambda i: (i, 0)),
                  pl.BlockSpec((tm, ka), lambda i: (i, 0))] + [_const_spec(c.shape) for c in consts],
        out_specs=pl.BlockSpec((tm, D_MODEL), lambda i: (i, 0)),
        out_shape=jax.ShapeDtypeStruct((t, D_MODEL), F32),
        compiler_params=_params(("parallel",)),
        name="post_ffn",
    )(x, a, *consts)


MLA_BLK = 128
ROPE_HALF = MLA_ROPE // 2


def _mla_proj_kernel(x_ref, cos_ref, sin_ref, gmix_ref, wd_ref, gqa_ref, gkva_ref, gkra_ref, gkrb_ref,
                     wuqa_ref, wuqb_ref, ga_ref, gb_ref, invn_ref, onesq_ref,
                     wuk_ref, gkn_ref, onesk_ref, wuv_ref,
                     qcat_ref, kcat_ref, v_ref, ckvf_ref, kr_ref):
    h = _rms(x_ref[...], gmix_ref[...]).astype(BF16)
    d = _dot(h, wd_ref[...])
    cq = _rms(d[:, :MLA_Q_LORA], gqa_ref[...]).astype(BF16)
    ckv = _rms(d[:, MLA_Q_LORA:MLA_Q_LORA + MLA_KV_LORA], gkva_ref[...])
    ckvf_ref[...] = ckv
    ckv_b = ckv.astype(BF16)
    cos = cos_ref[...]
    sin = sin_ref[...]
    base = MLA_Q_LORA + MLA_KV_LORA
    ka = d[:, base:base + MLA_BLK]
    kb = d[:, base + MLA_BLK:base + 2 * MLA_BLK]
    rk = lax.rsqrt(jnp.sum(ka * ka, axis=-1, keepdims=True) * (1.0 / MLA_ROPE) + EPS)
    kr = rk * (ka * gkra_ref[...] * cos + kb * gkrb_ref[...] * sin)
    kr_ref[...] = kr
    ya = _dot(cq, wuqa_ref[...])
    yb = _dot(cq, wuqb_ref[...])
    reps = MXU_DIM // MLA_BLK
    cos2 = jnp.tile(cos, (1, reps))
    sin2 = jnp.tile(sin, (1, reps))
    kr2 = jnp.tile(kr, (1, reps))
    onesq = onesq_ref[...]
    onesk = onesk_ref[...]
    kk = _dot(ckv_b, wuk_ref[...])
    for c in range(MLA_HEADS * MLA_BLK // MXU_DIM):
        sl = slice(c * MXU_DIM, (c + 1) * MXU_DIM)
        yac = ya[:, sl]
        ss = _dot((yac * yac).astype(BF16), onesq)
        r = lax.rsqrt(ss * invn_ref[:, sl] + EPS)
        qc = r * (yac * ga_ref[:, sl] * cos2 + yb[:, sl] * gb_ref[:, sl] * sin2)
        qcat_ref[:, sl] = qc.astype(BF16)
        kkc = kk[:, sl]
        ssk = _dot((kkc * kkc).astype(BF16), onesk)
        kn = kkc * lax.rsqrt(ssk * (1.0 / MLA_NOPE) + EPS) * gkn_ref[:, sl]
        kcat_ref[:, sl] = (kn + kr2).astype(BF16)
    v_ref[...] = _dot(ckv_b, wuv_ref[...]).astype(BF16)


def _head_block(rope_x1, rope_x2, nope):
    lead = (rope_x1 if rope_x1 is not None else nope).shape[:-1]
    z = lambda n: jnp.zeros(lead + (n,), F32)
    parts = [rope_x1 if rope_x1 is not None else z(ROPE_HALF),
             rope_x2 if rope_x2 is not None else z(ROPE_HALF),
             nope if nope is not None else z(MLA_NOPE),
             z(MLA_BLK - MLA_ROPE - MLA_NOPE)]
    return jnp.concatenate(parts, axis=-1)


def _rope_tables(pos):
    inv = ROPE_THETA ** (-jnp.arange(0, MLA_ROPE, 2, dtype=F32) / MLA_ROPE)
    ang = pos.astype(F32)[:, None] * inv[None, :]
    cs, sn = jnp.cos(ang), jnp.sin(ang)
    return _head_block(cs, cs, jnp.ones((pos.shape[0], MLA_NOPE), F32)), _head_block(-sn, sn, None)


def _mla_proj_consts(gmix, w_down, g_qa, g_kva, w_uq, w_uk, w_uv, g_qn, g_qr, g_kn, g_kr):
    hb = MLA_HEADS * MLA_BLK
    wd = w_down.astype(F32)
    base = MLA_Q_LORA + MLA_KV_LORA
    kx1, kx2 = wd[:, base:base + ROPE_HALF], wd[:, base + ROPE_HALF:base + MLA_ROPE]
    wd_ext = jnp.concatenate([wd[:, :base], _head_block(kx1, kx2, None), _head_block(kx2, kx1, None)],
                             axis=1).astype(BF16)
    g1, g2 = g_kr[:ROPE_HALF].astype(F32), g_kr[ROPE_HALF:].astype(F32)
    gkr_a = _head_block(g1, g2, None).reshape(1, MLA_BLK)
    gkr_b = _head_block(g2, g1, None).reshape(1, MLA_BLK)
    wq = w_uq.astype(F32).reshape(MLA_Q_LORA, MLA_HEADS, MLA_NOPE + MLA_ROPE)
    qn_w, q1_w, q2_w = wq[..., :MLA_NOPE], wq[..., MLA_NOPE:MLA_NOPE + ROPE_HALF], wq[..., MLA_NOPE + ROPE_HALF:]
    wuq_a = _head_block(q1_w, q2_w, qn_w).reshape(MLA_Q_LORA, hb).astype(BF16)
    wuq_b = _head_block(q2_w, q1_w, None).reshape(MLA_Q_LORA, hb).astype(BF16)
    r1, r2 = g_qr[:ROPE_HALF].astype(F32), g_qr[ROPE_HALF:].astype(F32)
    ga = jnp.tile(_head_block(r1, r2, g_qn.astype(F32)), MLA_HEADS).reshape(1, hb) * (MLA_SCALE * LOG2E)
    gb = jnp.tile(_head_block(r2, r1, None), MLA_HEADS).reshape(1, hb) * (MLA_SCALE * LOG2E)
    invn = jnp.tile(_head_block(jnp.full((ROPE_HALF,), 1.0 / MLA_ROPE, F32),
                                jnp.full((ROPE_HALF,), 1.0 / MLA_ROPE, F32),
                                jnp.full((MLA_NOPE,), 1.0 / MLA_NOPE, F32)), MLA_HEADS).reshape(1, hb)
    blk_groups = lambda groups: [(a + o, b + o) for o in range(0, MXU_DIM, MLA_BLK) for a, b in groups]
    ones_q = _group_ones(MXU_DIM, blk_groups([(0, MLA_ROPE), (MLA_ROPE, MLA_ROPE + MLA_NOPE)]))
    ones_k = _group_ones(MXU_DIM, blk_groups([(MLA_ROPE, MLA_ROPE + MLA_NOPE)]))
    wuk = _head_block(None, None, w_uk.astype(F32)).reshape(MLA_KV_LORA, hb).astype(BF16)
    gkn = jnp.tile(_head_block(None, None, g_kn.astype(F32)), MLA_HEADS).reshape(1, hb)
    wuv = w_uv.reshape(MLA_KV_LORA, MLA_HEADS * MLA_V).astype(BF16)
    return (gmix.reshape(1, D_MODEL).astype(F32), wd_ext,
            g_qa.reshape(1, -1).astype(F32), g_kva.reshape(1, -1).astype(F32), gkr_a, gkr_b,
            wuq_a, wuq_b, ga, gb, invn, ones_q, wuk, gkn, ones_k, wuv)


def _mla_project(x, pos, consts):
    t = x.shape[0]
    tm = _pick_tile(t)
    hb = MLA_HEADS * MLA_BLK
    cos_t, sin_t = _rope_tables(pos)
    row = lambda n: pl.BlockSpec((tm, n), lambda i: (i, 0))
    return pl.pallas_call(
        _mla_proj_kernel,
        grid=(t // tm,),
        in_specs=[row(D_MODEL), row(MLA_BLK), row(MLA_BLK)] + [_const_spec(c.shape) for c in consts],
        out_specs=[row(hb), row(hb), row(MLA_HEADS * MLA_V), row(MLA_KV_LORA), row(MLA_BLK)],
        out_shape=[jax.ShapeDtypeStruct((t, hb), BF16), jax.ShapeDtypeStruct((t, hb), BF16),
                   jax.ShapeDtypeStruct((t, MLA_HEADS * MLA_V), BF16),
                   jax.ShapeDtypeStruct((t, MLA_KV_LORA), F32),
                   jax.ShapeDtypeStruct((t, MLA_BLK), F32)],
        compiler_params=_params(("parallel",)),
        name="mla_proj",
    )(x, cos_t, sin_t, *consts)


def _mla_attn_kernel(q_ref, k_ref, v_ref, o_ref, m_sc, l_sc, acc_sc, *, tq):
    i = pl.program_id(2)
    rows = 2 * tq
    _softmax_init(m_sc, l_sc, acc_sc)

    def step(j, masked):
        off = pl.multiple_of(j * tq, tq)
        s = jnp.concatenate(
            [_dot_nt(q_ref[:, a * LANES:(a + 1) * LANES], k_ref[pl.ds(off, tq), a * LANES:(a + 1) * LANES])
             for a in range(2)], axis=0)
        if masked:
            r = lax.broadcasted_iota(jnp.int32, (rows, tq), 0) & (tq - 1)
            c = lax.broadcasted_iota(jnp.int32, (rows, tq), 1)
            s = jnp.where(c <= r, s, NEG)
        _softmax_update(s, v_ref[pl.ds(off, tq), :], m_sc, l_sc, acc_sc)

    def body(j, carry):
        step(j, False)
        return carry

    lax.fori_loop(0, i, body, 0)
    step(i, True)
    a = acc_sc[...] / l_sc[...]
    lane = lax.broadcasted_iota(jnp.int32, (tq, LANES), 1)
    o_ref[...] = jnp.where(lane < MLA_V, a[:tq], a[tq:]).astype(BF16)


def _mla_attend_prompt(qcat, kcat, v, batch, seq):
    tq = _pick_tile(seq, (512, 256, 128))
    nq = seq // tq
    kern = functools.partial(_mla_attn_kernel, tq=tq)
    return pl.pallas_call(
        kern,
        grid=(batch, MLA_HEADS // 2, nq),
        in_specs=[pl.BlockSpec((tq, 2 * MLA_BLK), lambda b, h, i: (b * nq + i, h)),
                  pl.BlockSpec((seq, 2 * MLA_BLK), lambda b, h, i: (b, h)),
                  pl.BlockSpec((seq, 2 * MLA_V), lambda b, h, i: (b, h))],
        out_specs=pl.BlockSpec((tq, 2 * MLA_V), lambda b, h, i: (b * nq + i, h)),
        out_shape=jax.ShapeDtypeStruct((batch * seq, MLA_HEADS * MLA_V), BF16),
        scratch_shapes=[pltpu.VMEM((2 * tq, LANES), F32)] * 3,
        compiler_params=_params(("parallel", "parallel", "arbitrary")),
        name="mla_attn_prompt",
    )(qcat, kcat, v)


def _mla_sample_kernel(pt_ref, qn_ref, qr_ref, cn_ref, rn_ref, wuk_ref, wukt_ref, gkn_ref, wuv_ref, *rest, n_pp):
    c_pages = rest[:n_pp]
    r_pages = rest[n_pp:2 * n_pp]
    o_ref = rest[2 * n_pp]
    lhs_sc, cbuf, rbuf, m_sc, l_sc, acc_sc = rest[2 * n_pp + 1:]
    del pt_ref
    c = pl.program_id(1)
    nw = MLA_HEADS * MLA_NOPE
    n_q = N_NEW * MLA_HEADS

    @pl.when(c == 0)
    def _():
        _softmax_init(m_sc, l_sc, acc_sc)
        qn = (qn_ref[...].astype(F32) * gkn_ref[...]).astype(BF16)
        lhs_sc[:nw, :] = wukt_ref[...]
        lhs_sc[nw:, :] = _dot_nt(qn, wuk_ref[...]).astype(BF16)

    head8 = lax.broadcasted_iota(jnp.int32, (MLA_HEADS, LANES), 0)

    def scores(cb, rt):
        out = _dot_nt(lhs_sc[...], cb)
        n = out.shape[1]
        ssq = jnp.zeros((MLA_HEADS, n), F32)
        for h in range(MLA_HEADS):
            kk = out[h * MLA_NOPE:(h + 1) * MLA_NOPE]
            ss = jnp.sum(kk * kk, axis=0, keepdims=True)
            ssq = jnp.where(jnp.tile(head8, (1, n // LANES)) == h, ss, ssq)
        rinv = lax.rsqrt(ssq * (1.0 / MLA_NOPE) + EPS)
        return out[nw:] * jnp.tile(rinv, (N_NEW, 1)) + _dot(qr_ref[...], rt)

    for r in range(n_pp):
        cbuf[r * PAGE_SIZE:(r + 1) * PAGE_SIZE, :] = c_pages[r][...].astype(BF16)
        rbuf[:, r * PAGE_SIZE:(r + 1) * PAGE_SIZE] = r_pages[r][...].astype(BF16)
    _softmax_update(scores(cbuf[...], rbuf[...]), cbuf[...], m_sc, l_sc, acc_sc)

    @pl.when(c == pl.num_programs(1) - 1)
    def _():
        row = lax.broadcasted_iota(jnp.int32, (n_q, LANES), 0)
        col = lax.broadcasted_iota(jnp.int32, (n_q, LANES), 1)
        cn = cn_ref[...].astype(BF16)
        s = jnp.where(col <= row // MLA_HEADS, scores(cn, rn_ref[...].astype(BF16)), NEG)
        _softmax_update(s, cn, m_sc, l_sc, acc_sc)
        inv = jnp.tile(1.0 / l_sc[...], (1, MLA_KV_LORA // LANES))
        o_lat = (acc_sc[...] * inv).astype(BF16)
        o2 = _dot(o_lat, wuv_ref[...])
        hrow = lax.broadcasted_iota(jnp.int32, (MLA_HEADS, MLA_HEADS * MLA_V), 0)
        hcol = lax.broadcasted_iota(jnp.int32, (MLA_HEADS, MLA_HEADS * MLA_V), 1) // MLA_V
        keep = hrow == hcol
        outs = [jnp.sum(jnp.where(keep, o2[t * MLA_HEADS:(t + 1) * MLA_HEADS, :], 0.0), axis=0, keepdims=True)
                for t in range(N_NEW)]
        o_ref[...] = jnp.concatenate(outs, axis=0)


def _mla_attend_sample(qcat_s, ckv_s, kr_s, cache_ckv, cache_kr, page_table, w_uk, g_kn, w_uv, layer):
    n_seq = qcat_s.shape[0] // N_NEW
    n_pages = page_table.shape[1]
    n_pp = math.gcd(PAGES_PER_STEP, n_pages)
    n_chunks = n_pages // n_pp
    nw = MLA_HEADS * MLA_NOPE
    n_q = N_NEW * MLA_HEADS
    q4 = qcat_s.reshape(n_seq, N_NEW, MLA_HEADS, MLA_BLK)
    q_rope = q4[..., :MLA_ROPE].reshape(n_seq, n_q, MLA_ROPE)
    q_nope = q4[..., MLA_ROPE:MLA_ROPE + MLA_NOPE]
    eye_h = jnp.eye(MLA_HEADS, dtype=BF16)
    qn_bd = jnp.einsum('bthd,hH->bthHd', q_nope, eye_h).reshape(n_seq, n_q, nw)
    c_new = jnp.pad(ckv_s.reshape(n_seq, N_NEW, MLA_KV_LORA), ((0, 0), (0, LANES - N_NEW), (0, 0)))
    r_new = jnp.pad(kr_s.reshape(n_seq, N_NEW, MLA_ROPE).transpose(0, 2, 1), ((0, 0), (0, 0), (0, LANES - N_NEW)))
    wuk = w_uk.reshape(MLA_KV_LORA, nw).astype(BF16)
    cache_kr_t = jnp.swapaxes(cache_kr, 2, 3)

    def page_spec(r, shape):
        return pl.BlockSpec((None, None) + shape,
                            lambda b, c, pt: (layer, pt[b * n_pages + c * n_pp + r], 0, 0))

    per_seq = lambda shape: pl.BlockSpec((None,) + shape, lambda b, c, pt: (b,) + (0,) * len(shape))
    const = lambda shape: pl.BlockSpec(shape, lambda b, c, pt: (0,) * len(shape))
    kern = functools.partial(_mla_sample_kernel, n_pp=n_pp)
    out = pl.pallas_call(
        kern,
        grid_spec=pltpu.PrefetchScalarGridSpec(
            num_scalar_prefetch=1,
            grid=(n_seq, n_chunks),
            in_specs=[per_seq((n_q, nw)), per_seq((n_q, MLA_ROPE)), per_seq((LANES, MLA_KV_LORA)),
                      per_seq((MLA_ROPE, LANES)), const((MLA_KV_LORA, nw)), const((nw, MLA_KV_LORA)),
                      const((1, nw)), const((MLA_KV_LORA, MLA_HEADS * MLA_V))]
                     + [page_spec(r, (PAGE_SIZE, MLA_KV_LORA)) for r in range(n_pp)]
                     + [page_spec(r, (MLA_ROPE, PAGE_SIZE)) for r in range(n_pp)],
            out_specs=per_seq((N_NEW, MLA_HEADS * MLA_V)),
            scratch_shapes=[pltpu.VMEM((nw + n_q, MLA_KV_LORA), BF16),
                            pltpu.VMEM((n_pp * PAGE_SIZE, MLA_KV_LORA), BF16),
                            pltpu.VMEM((MLA_ROPE, n_pp * PAGE_SIZE), BF16),
                            pltpu.VMEM((n_q, LANES), F32), pltpu.VMEM((n_q, LANES), F32),
                            pltpu.VMEM((n_q, MLA_KV_LORA), F32)]),
        out_shape=jax.ShapeDtypeStruct((n_seq, N_NEW, MLA_HEADS * MLA_V), F32),
        compiler_params=_params(("parallel", "arbitrary")),
        name="mla_attn_sample",
    )(page_table.reshape(-1), qn_bd, q_rope, c_new, r_new, wuk, wuk.T,
      jnp.tile(g_kn.astype(F32), MLA_HEADS).reshape(1, nw),
      w_uv.reshape(MLA_KV_LORA, MLA_HEADS * MLA_V).astype(BF16),
      *([cache_ckv] * n_pp), *([cache_kr_t] * n_pp))
    return out.reshape(n_seq * N_NEW, MLA_HEADS * MLA_V).astype(BF16)


def _hg_proj_kernel(x_ref, gmix_ref, w_ref, lbl_ref, q_ref, k_ref, lf_ref, v_ref, g_ref, *, layer):
    h = _rms(x_ref[...], gmix_ref[...]).astype(BF16)
    z = _dot(h, w_ref[...])
    lg = lbl_ref[...]
    e = jnp.exp(lg - jnp.max(lg, axis=0, keepdims=True))
    sm = e / jnp.sum(e, axis=0, keepdims=True)
    lb = jnp.sum(sm[:layer + 1], axis=0, keepdims=True) - sm[0:1]
    nk = HG_HEADS * HG_DK
    zq = z[:, :nk]
    q_ref[...] = zq * jax.nn.sigmoid(zq)
    f = lb + (1.0 - lb) * jax.nn.sigmoid(z[:, nk:2 * nk])
    k_ref[...] = 1.0 - f
    lf_ref[...] = jnp.log(f)
    v_ref[...] = z[:, 2 * nk:3 * nk]
    g_ref[...] = z[:, 3 * nk:]


def _hg_project(x, consts, layer):
    t = x.shape[0]
    tm = _pick_tile(t)
    nk = HG_HEADS * HG_DK
    row = lambda n: pl.BlockSpec((tm, n), lambda i: (i, 0))
    kern = functools.partial(_hg_proj_kernel, layer=layer)
    return pl.pallas_call(
        kern,
        grid=(t // tm,),
        in_specs=[row(D_MODEL)] + [_const_spec(c.shape) for c in consts],
        out_specs=[row(nk)] * 5,
        out_shape=[jax.ShapeDtypeStruct((t, nk), F32)] * 5,
        compiler_params=_params(("parallel",)),
        name="hg_proj",
    )(x, *consts)


def _hg_out(o, g, gout):
    return (_rms(o, gout) * (g * jax.nn.sigmoid(g))).astype(BF16)


def _hg_level_tables(chunk):
    r = jnp.arange(chunk)[:, None]
    u = jnp.arange(chunk)[None, :]
    mats, masks = [], []
    z = chunk // 2
    while z >= 1:
        ref = (r // (2 * z)) * (2 * z) + z - 1
        mats.append(jnp.where(r > ref, (u > ref) & (u <= r), (u > r) & (u <= ref)))
        masks.append((r // (2 * z) == u // (2 * z)) & (r % (2 * z) >= z) & (u % (2 * z) < z))
        z //= 2
    mats.append(u <= r)
    masks.append(u == r)
    mstack = jnp.concatenate(mats, axis=0).astype(BF16)
    return jnp.concatenate([mstack, mstack], axis=1), jnp.stack(masks).astype(F32)


def _hg_chunk_kernel(q_ref, k_ref, lf_ref, v_ref, g_ref, gout_ref, mstack_ref, mask_ref,
                     o_ref, sfin_ref, s_sc, *, chunk):
    n = pl.program_id(1)

    @pl.when(n == 0)
    def _():
        s_sc[...] = jnp.zeros(s_sc.shape, F32)

    n_lev = mask_ref.shape[0] - 1
    lf = lf_ref[...]
    hi = lf.astype(BF16)
    lo = (lf - hi.astype(F32)).astype(BF16)
    dec = _dot(mstack_ref[...], jnp.concatenate([hi, lo], axis=0))

    for h in range(HG_HEADS):
        sl = slice(h * HG_DK, (h + 1) * HG_DK)
        q = q_ref[:, sl]
        k = k_ref[:, sl]
        vb = v_ref[:, sl].astype(BF16)
        a = _dot_nt(q.astype(BF16), k.astype(BF16)) * mask_ref[n_lev]
        for lev in range(n_lev):
            gfac = jnp.exp(dec[lev * chunk:(lev + 1) * chunk, sl])
            a = a + _dot_nt((q * gfac).astype(BF16), (k * gfac).astype(BF16)) * mask_ref[lev]
        cu = dec[n_lev * chunk:, sl]
        state = s_sc[h]
        o = _dot(a.astype(BF16), vb) + _dot((q * jnp.exp(cu)).astype(BF16), state.astype(BF16))
        o_ref[:, sl] = _hg_out(o, g_ref[:, sl], gout_ref[...])
        last = cu[chunk - 1:chunk, :]
        k_dec = (k * jnp.exp(last - cu)).T.astype(BF16)
        s_sc[h] = state * _col_replicated(jnp.exp(last)) + _dot(k_dec, vb)

    @pl.when(n == pl.num_programs(1) - 1)
    def _():
        sfin_ref[...] = s_sc[...]


def _hg_recur_prompt(q, k, lf, v, g, g_out, batch, seq):
    chunk = HG_CHUNK
    nc = seq // chunk
    nk = HG_HEADS * HG_DK
    mstack, masks = _hg_level_tables(chunk)
    blk = pl.BlockSpec((chunk, nk), lambda b, n: (b * nc + n, 0))
    kern = functools.partial(_hg_chunk_kernel, chunk=chunk)
    return pl.pallas_call(
        kern,
        grid=(batch, nc),
        in_specs=[blk] * 5 + [_const_spec((1, HG_DV)), _const_spec(mstack.shape), _const_spec(masks.shape)],
        out_specs=[blk, pl.BlockSpec((None, HG_HEADS, HG_DK, HG_DV), lambda b, n: (b, 0, 0, 0))],
        out_shape=[jax.ShapeDtypeStruct((batch * seq, nk), BF16),
                   jax.ShapeDtypeStruct((batch, HG_HEADS, HG_DK, HG_DV), F32)],
        scratch_shapes=[pltpu.VMEM((HG_HEADS, HG_DK, HG_DV), F32)],
        compiler_params=_params(("parallel", "arbitrary")),
        name="hg_recur_prompt",
    )(q, k, lf, v, g, g_out.reshape(1, HG_DV).astype(F32), mstack, masks)


def _hg_sample_kernel(q_ref, k_ref, lf_ref, v_ref, g_ref, gout_ref, s0_ref, o_ref, s_ref, *, n_blk):
    row8 = lax.broadcasted_iota(jnp.int32, (8, LANES), 0)

    def rows_to_tile(rows):
        top = jnp.zeros((8, LANES), F32)
        for t, r in enumerate(rows):
            top = jnp.where(row8 == t, r, top)
        return jnp.concatenate([top, jnp.zeros((LANES - 8, LANES), F32)], axis=0)

    def per_seq(b, carry):
        for h in range(HG_HEADS):
            sl = slice(h * HG_DK, (h + 1) * HG_DK)
            q = [q_ref[b, t:t + 1, sl] for t in range(N_NEW)]
            k = [k_ref[b, t:t + 1, sl] for t in range(N_NEW)]
            v = [v_ref[b, t:t + 1, sl] for t in range(N_NEW)]
            cum = []
            for t in range(N_NEW):
                lf_t = lf_ref[b, t:t + 1, sl]
                cum.append(lf_t if t == 0 else cum[-1] + lf_t)
            s0 = s0_ref[b, h]
            q_dec = rows_to_tile([q[t] * jnp.exp(cum[t]) for t in range(N_NEW)])[:16]
            inter = _dot(q_dec.astype(BF16), s0.astype(BF16))
            for t in range(N_NEW):
                o = inter[t:t + 1]
                for s in range(t + 1):
                    a = jnp.sum(q[t] * k[s] * jnp.exp(cum[t] - cum[s]), axis=-1, keepdims=True)
                    o = o + a * v[s]
                o_ref[b, t:t + 1, sl] = _rms(o, gout_ref[...]) * (
                    g_ref[b, t:t + 1, sl] * jax.nn.sigmoid(g_ref[b, t:t + 1, sl]))
            last = cum[-1]
            k_dec = rows_to_tile([k[s] * jnp.exp(last - cum[s]) for s in range(N_NEW)])
            v_tile = rows_to_tile(v)
            s_ref[b, h] = (s0 * _col_replicated(jnp.exp(last))
                           + _dot(k_dec.T.astype(BF16), v_tile.astype(BF16)))
        return carry

    lax.fori_loop(0, n_blk, per_seq, 0)


def _hg_recur_sample(q, k, lf, v, g, g_out, state):
    n_seq, _, nk = q.shape
    n_blk = _pick_tile(n_seq, (HG_SEQ_BLOCK, 4, 2, 1))
    tok = pl.BlockSpec((n_blk, N_NEW, nk), lambda i: (i, 0, 0))
    st = pl.BlockSpec((n_blk, HG_HEADS, HG_DK, HG_DV), lambda i: (i, 0, 0, 0))
    kern = functools.partial(_hg_sample_kernel, n_blk=n_blk)
    return pl.pallas_call(
        kern,
        grid=(n_seq // n_blk,),
        in_specs=[tok] * 5 + [pl.BlockSpec((1, HG_DV), lambda i: (0, 0)), st],
        out_specs=[tok, st],
        out_shape=[jax.ShapeDtypeStruct((n_seq, N_NEW, nk), F32),
                   jax.ShapeDtypeStruct(state.shape, F32)],
        compiler_params=_params(("parallel",)),
        name="hg_recur_sample",
    )(q, k, lf, v, g, g_out.reshape(1, HG_DV).astype(F32), state.astype(F32))


def kernel(x_prompt, x_sample, cache_da_k, cache_da_v, cache_mla_ckv, cache_mla_krope, state_hgrn, page_table, norm_mix, norm_ffn, ffn_w_gu, ffn_w_down, da_w_qkv, da_q_gain, da_k_gain, da_lambda, da_sub_gain, da_w_o, mla_w_down, mla_qa_gain, mla_kva_gain, mla_w_uq, mla_w_uk, mla_w_uv, mla_qn_gain, mla_qr_gain, mla_kn_gain, mla_kr_gain, mla_w_o, hg_w_in, hg_lb_logits, hg_out_gain, hg_w_o):
    batch, seq, _ = x_prompt.shape
    n_seq, n_new, _ = x_sample.shape
    assert n_new == N_NEW and seq % HG_CHUNK == 0
    tp, ts = batch * seq, n_seq * n_new
    past = page_table.shape[1] * PAGE_SIZE
    depth = norm_mix.shape[0]
    xp = x_prompt.reshape(tp, D_MODEL).astype(F32)
    xs = x_sample.reshape(ts, D_MODEL).astype(F32)
    pos_p = jnp.tile(jnp.arange(seq), batch)
    pos_s = jnp.tile(past + jnp.arange(n_new), n_seq)

    da_kp, da_vp, da_ks, da_vs = [], [], [], []
    mla_cp, mla_rp, mla_cs, mla_rs = [], [], [], []
    hg_p, hg_s = [], []
    for i in range(depth):
        kind, j = i % N_MIXERS, i // N_MIXERS
        if kind == 0:
            lam_init = 0.8 - 0.6 * math.exp(-0.3 * i)
            lam_p = da_lambda[j].astype(F32)
            consts = _da_proj_consts(norm_mix[i], da_w_qkv[j], da_q_gain[j], da_k_gain[j])
            q_p, kf_p, vf_p, kb_p, vb_p = _da_project(xp, consts, seq)
            q_s, kf_s, vf_s, kb_s, vb_s = _da_project(xs, consts)
            nq_blocks = seq // _pick_tile(seq, (512, 256, 128))
            if n_seq == batch * DA_KV_HEADS * nq_blocks and page_table.shape[1] % (2 * DA_GROUP) == 0:
                a_p, a_s = _da_attend_fused(q_p, kb_p, vb_p, q_s, kb_s, vb_s, cache_da_k, cache_da_v,
                                            page_table, lam_p, da_sub_gain[j], lam_init, j, batch, seq)
            else:
                a_p = _da_attend_prompt(q_p, kb_p, vb_p, lam_p, da_sub_gain[j], lam_init, batch, seq)
                a_s = _da_attend_sample(q_s, kb_s, vb_s, cache_da_k, cache_da_v, page_table,
                                        lam_p, da_sub_gain[j], lam_init, j)
            w_o = da_w_o[j]
            da_kp.append(kf_p.reshape(batch, DA_KV_HEADS, 2, DA_HEAD, seq).transpose(0, 4, 1, 2, 3))
            da_vp.append(vf_p.reshape(batch, seq, DA_KV_HEADS, 2 * DA_HEAD))
            da_ks.append(kf_s.reshape(n_seq, n_new, DA_KV_HEADS, 2, DA_HEAD))
            da_vs.append(vf_s.reshape(n_seq, n_new, DA_KV_HEADS, 2 * DA_HEAD))
        elif kind == 1:
            consts = _mla_proj_consts(norm_mix[i], mla_w_down[j], mla_qa_gain[j], mla_kva_gain[j], mla_w_uq[j],
                                      mla_w_uk[j], mla_w_uv[j], mla_qn_gain[j], mla_qr_gain[j],
                                      mla_kn_gain[j], mla_kr_gain[j])
            qcat_p, kcat_p, v_p, ckv_p, kr_p = _mla_project(xp, pos_p, consts)
            qcat_s, _, _, ckv_s, kr_s = _mla_project(xs, pos_s, consts)
            kr_p, kr_s = kr_p[:, :MLA_ROPE], kr_s[:, :MLA_ROPE]
            a_p = _mla_attend_prompt(qcat_p, kcat_p, v_p, batch, seq)
            a_s = _mla_attend_sample(qcat_s, ckv_s, kr_s, cache_mla_ckv, cache_mla_krope,
                                     page_table, mla_w_uk[j], mla_kn_gain[j], mla_w_uv[j], j)
            w_o = mla_w_o[j]
            mla_cp.append(ckv_p.reshape(batch, seq, MLA_KV_LORA))
            mla_rp.append(kr_p.reshape(batch, seq, MLA_ROPE))
            mla_cs.append(ckv_s.reshape(n_seq, n_new, MLA_KV_LORA))
            mla_rs.append(kr_s.reshape(n_seq, n_new, MLA_ROPE))
        else:
            consts = (norm_mix[i].reshape(1, D_MODEL).astype(F32), hg_w_in[j].astype(BF16),
                      hg_lb_logits.astype(F32))
            hq, hk, hlf, hv, hgate = _hg_project(xp, consts, i)
            a_p, s_p = _hg_recur_prompt(hq, hk, hlf, hv, hgate, hg_out_gain[j], batch, seq)
            smp = lambda a: a.reshape(n_seq, n_new, HG_HEADS * HG_DK)
            a_s, s_s = _hg_recur_sample(*[smp(a) for a in _hg_project(xs, consts, i)],
                                        hg_out_gain[j], state_hgrn[j])
            a_s = a_s.reshape(ts, HG_HEADS * HG_DV).astype(BF16)
            w_o = hg_w_o[j]
            hg_p.append(s_p.astype(state_hgrn.dtype))
            hg_s.append(s_s.astype(state_hgrn.dtype))
        post_consts = (w_o.astype(BF16), norm_ffn[i].reshape(1, D_MODEL).astype(F32),
                       ffn_w_gu[i].astype(BF16), ffn_w_down[i].astype(BF16))
        xp = _post(xp, a_p, post_consts)
        xs = _post(xs, a_s, post_consts)

    return (xp.reshape(batch, seq, D_MODEL), xs.reshape(n_seq, n_new, D_MODEL),
            jnp.stack(da_kp), jnp.stack(da_vp), jnp.stack(da_ks), jnp.stack(da_vs),
            jnp.stack(mla_cp), jnp.stack(mla_rp), jnp.stack(mla_cs), jnp.stack(mla_rs),
            jnp.stack(hg_p), jnp.stack(hg_s))
```

```python
import functools
import math

import jax
import jax.numpy as jnp
from jax import lax
from jax.experimental import pallas as pl
from jax.experimental.pallas import tpu as pltpu

F32 = jnp.float32
BF16 = jnp.bfloat16

D_MODEL = 1024
PAGE_SIZE = 128
N_MIXERS = 3
N_NEW = 4
DA_HEADS = 8
DA_KV_HEADS = 4
DA_GROUP = DA_HEADS // DA_KV_HEADS
DA_HEAD = 64
MLA_HEADS = 8
MLA_NOPE = 64
MLA_ROPE = 32
MLA_V = 64
MLA_Q_LORA = 256
MLA_KV_LORA = 256
MLA_SCALE = (MLA_NOPE + MLA_ROPE) ** -0.5
ROPE_THETA = 10000.0
HG_HEADS = 8
HG_DK = 128
HG_DV = 128
D_FF = 2816
EPS = 1e-6

LOG2E = math.log2(math.e)
NEG = -1e30
LANES = 128
MXU_DIM = 256
VMEM_LIMIT = 56 * 1024 * 1024
PAGES_PER_STEP = 16
MLA_PAGES_PER_STEP = 32
HG_CHUNK = 128
HG_SEQ_BLOCK = 8


def _pick_tile(n, candidates=(512, 256, 128, 64, 32, 16, 8)):
    for c in candidates:
        if n % c == 0:
            return c
    raise ValueError(f"no tile divides {n}")


def _const_spec(shape):
    nd = len(shape)
    return pl.BlockSpec(shape, lambda *_: (0,) * nd, pipeline_mode=pl.Buffered(1))


def _params(sem):
    return pltpu.CompilerParams(dimension_semantics=sem, vmem_limit_bytes=VMEM_LIMIT)


def _rms(x, g):
    ms = jnp.mean(x * x, axis=-1, keepdims=True)
    return x * lax.rsqrt(ms + EPS) * g


def _dot(a, b):
    return jnp.dot(a, b, preferred_element_type=F32)


def _dot_nt(a, b):
    return lax.dot_general(a, b, (((1,), (1,)), ((), ())), preferred_element_type=F32)


def _group_ones(width, groups):
    idx = jnp.arange(width)
    gid = jnp.full((width,), -1, jnp.int32)
    for n, (a, b) in enumerate(groups):
        gid = jnp.where((idx >= a) & (idx < b), n, gid)
    m = (gid[:, None] == gid[None, :]) & (gid[:, None] >= 0)
    return m.astype(BF16)


def _lambda_value(lv, lam_init):
    a = jnp.sum(lv[0:1] * lv[1:2], axis=-1, keepdims=True)
    b = jnp.sum(lv[2:3] * lv[3:4], axis=-1, keepdims=True)
    return jnp.exp(a) - jnp.exp(b) + lam_init


def _col_replicated(row):
    return jnp.broadcast_to(row, (LANES, LANES)).T


def _softmax_init(m_sc, l_sc, acc_sc):
    m_sc[...] = jnp.full(m_sc.shape, NEG, F32)
    l_sc[...] = jnp.zeros(l_sc.shape, F32)
    acc_sc[...] = jnp.zeros(acc_sc.shape, F32)


def _softmax_update(s, vb, m_sc, l_sc, acc_sc):
    n = s.shape[1]
    m_prev = m_sc[...]
    m_new = jnp.maximum(m_prev, jnp.max(s, axis=-1, keepdims=True))
    alpha = jnp.exp2(m_prev - m_new)
    p = jnp.exp2(s - jnp.tile(m_new, (1, n // LANES)))
    l_sc[...] = alpha * l_sc[...] + jnp.sum(p, axis=-1, keepdims=True)
    m_sc[...] = m_new
    acc_sc[...] = (jnp.tile(alpha, (1, acc_sc.shape[1] // LANES)) * acc_sc[...]
                   + _dot(p.astype(BF16), vb))


def _da_proj_kernel(x_ref, gmix_ref, w_ref, gq_ref, gk_ref, ones_ref,
                    q_ref, kf_ref, vf_ref, kb_ref, vb_ref, *, cache_layout):
    tm = x_ref.shape[0]
    h = _rms(x_ref[...], gmix_ref[...]).astype(BF16)
    y = _dot(h, w_ref[...])
    ones = ones_ref[...]

    def group_norm(yc, g):
        ss = _dot((yc * yc).astype(BF16), ones)
        return yc * lax.rsqrt(ss * (1.0 / DA_HEAD) + EPS) * g

    nq = DA_HEADS * 2 * DA_HEAD
    nk = DA_KV_HEADS * 2 * DA_HEAD
    for c in range(nq // MXU_DIM):
        sl = slice(c * MXU_DIM, (c + 1) * MXU_DIM)
        q_ref[:, sl] = group_norm(y[:, sl], gq_ref[:, sl]).astype(BF16)
    for c in range(nk // MXU_DIM):
        sl = slice(c * MXU_DIM, (c + 1) * MXU_DIM)
        kn = group_norm(y[:, nq + c * MXU_DIM:nq + (c + 1) * MXU_DIM], gk_ref[:, sl])
        if cache_layout:
            kf_ref[sl, :] = kn.T
        else:
            kf_ref[:, sl] = kn
        kb_ref[:, sl] = kn.astype(BF16)
    v = y[:, nq + nk:]
    if cache_layout:
        for hd in range(DA_KV_HEADS):
            vf_ref[pl.ds(hd, tm, stride=DA_KV_HEADS), :] = v[:, hd * LANES:(hd + 1) * LANES]
    else:
        vf_ref[...] = v
    vb_ref[...] = v.astype(BF16)


def _da_proj_consts(gmix, w_qkv, g_q, g_k):
    nq = DA_HEADS * 2 * DA_HEAD
    nk = DA_KV_HEADS * 2 * DA_HEAD
    gq_row = (jnp.tile(g_q.astype(F32), nq // DA_HEAD) * (DA_HEAD ** -0.5 * LOG2E)).reshape(1, nq)
    gk_row = jnp.tile(g_k.astype(F32), nk // DA_HEAD).reshape(1, nk)
    ones = _group_ones(MXU_DIM, [(i * DA_HEAD, (i + 1) * DA_HEAD) for i in range(MXU_DIM // DA_HEAD)])
    return (gmix.reshape(1, D_MODEL).astype(F32), w_qkv.astype(BF16), gq_row, gk_row, ones)


def _da_project(x, consts, seq=None):
    t = x.shape[0]
    tm = _pick_tile(t if seq is None else seq)
    nq = DA_HEADS * 2 * DA_HEAD
    nk = DA_KV_HEADS * 2 * DA_HEAD
    row = lambda n: pl.BlockSpec((tm, n), lambda i: (i, 0))
    if seq is None:
        kv_specs = [row(nk), row(nk)]
        kv_shapes = [jax.ShapeDtypeStruct((t, nk), F32), jax.ShapeDtypeStruct((t, nk), F32)]
    else:
        per = seq // tm
        kv_specs = [pl.BlockSpec((None, nk, tm), lambda i: (i // per, 0, i % per)),
                    pl.BlockSpec((DA_KV_HEADS * tm, 2 * DA_HEAD), lambda i: (i, 0))]
        kv_shapes = [jax.ShapeDtypeStruct((t // seq, nk, seq), F32),
                     jax.ShapeDtypeStruct((t * DA_KV_HEADS, 2 * DA_HEAD), F32)]
    kern = functools.partial(_da_proj_kernel, cache_layout=seq is not None)
    return pl.pallas_call(
        kern,
        grid=(t // tm,),
        in_specs=[row(D_MODEL)] + [_const_spec(c.shape) for c in consts],
        out_specs=[row(nq)] + kv_specs + [row(nk), row(nk)],
        out_shape=[jax.ShapeDtypeStruct((t, nq), BF16)] + kv_shapes
                  + [jax.ShapeDtypeStruct((t, nk), BF16), jax.ShapeDtypeStruct((t, nk), BF16)],
        compiler_params=_params(("parallel",)),
        name="da_proj",
    )(x, *consts)


def _da_attn_kernel(lam_ref, q_ref, k_ref, v_ref, gsub_ref, o_ref,
                    qs_sc, m_sc, l_sc, acc_sc, *, tq, lam_init):
    i = pl.program_id(2)
    rows = 2 * DA_GROUP * tq
    q = q_ref[...]
    lane = lax.broadcasted_iota(jnp.int32, (tq, LANES), 1)
    first_map = lane < DA_HEAD
    zero = jnp.zeros((tq, LANES), BF16)
    for g in range(DA_GROUP):
        qg = q[:, g * LANES:(g + 1) * LANES]
        qs_sc[(2 * g) * tq:(2 * g + 1) * tq, :] = jnp.where(first_map, qg, zero)
        qs_sc[(2 * g + 1) * tq:(2 * g + 2) * tq, :] = jnp.where(first_map, zero, qg)
    _softmax_init(m_sc, l_sc, acc_sc)

    def step(j, masked):
        off = pl.multiple_of(j * tq, tq)
        s = _dot_nt(qs_sc[...], k_ref[pl.ds(off, tq), :])
        if masked:
            r = lax.broadcasted_iota(jnp.int32, (rows, tq), 0) & (tq - 1)
            c = lax.broadcasted_iota(jnp.int32, (rows, tq), 1)
            s = jnp.where(c <= r, s, NEG)
        _softmax_update(s, v_ref[pl.ds(off, tq), :], m_sc, l_sc, acc_sc)

    def body(jj, carry):
        step(2 * jj, False)
        step(2 * jj + 1, False)
        return carry

    lax.fori_loop(0, i // 2, body, 0)

    @pl.when(i % 2 == 1)
    def _():
        step(i - 1, False)

    step(i, True)

    lam = _lambda_value(lam_ref[...], lam_init)
    a = acc_sc[...] / l_sc[...]
    for g in range(DA_GROUP):
        o = a[(2 * g) * tq:(2 * g + 1) * tq] - lam * a[(2 * g + 1) * tq:(2 * g + 2) * tq]
        on = _rms(o, gsub_ref[...]) * (1.0 - lam_init)
        o_ref[:, g * LANES:(g + 1) * LANES] = on.astype(BF16)


def _da_attend_prompt(q, kb, vb, lam_p, g_sub, lam_init, batch, seq):
    tq = _pick_tile(seq, (512, 256, 128))
    nq = seq // tq
    rows = 2 * DA_GROUP * tq
    kern = functools.partial(_da_attn_kernel, tq=tq, lam_init=lam_init)
    return pl.pallas_call(
        kern,
        grid=(batch, DA_KV_HEADS, nq),
        in_specs=[_const_spec(lam_p.shape),
                  pl.BlockSpec((tq, 2 * LANES), lambda b, h, i: (b * nq + i, h)),
                  pl.BlockSpec((seq, LANES), lambda b, h, i: (b, h)),
                  pl.BlockSpec((seq, LANES), lambda b, h, i: (b, h)),
                  _const_spec((1, LANES))],
        out_specs=pl.BlockSpec((tq, 2 * LANES), lambda b, h, i: (b * nq + i, h)),
        out_shape=jax.ShapeDtypeStruct((batch * seq, DA_HEADS * 2 * DA_HEAD), BF16),
        scratch_shapes=[pltpu.VMEM((rows, LANES), BF16), pltpu.VMEM((rows, LANES), F32),
                        pltpu.VMEM((rows, LANES), F32), pltpu.VMEM((rows, LANES), F32)],
        compiler_params=_params(("parallel", "parallel", "arbitrary")),
        name="da_attn_prompt",
    )(lam_p, q, kb, vb, g_sub.reshape(1, LANES).astype(F32))


def _da_sample_chunk(c, n_chunks, lam_ref, qbd_ref, kn_ref, vn_ref, gsub_ref, k_pages, v_pages, o_ref,
                     kbuf, vbuf, m_sc, l_sc, acc_sc, lam_init):
    n_pp = len(k_pages)

    @pl.when(c == 0)
    def _():
        _softmax_init(m_sc, l_sc, acc_sc)

    qbd = qbd_ref[...]
    for r in range(n_pp):
        cols = slice(r * PAGE_SIZE, (r + 1) * PAGE_SIZE)
        kbuf[:, cols] = k_pages[r][...].astype(BF16)
        for h in range(DA_KV_HEADS):
            vbuf[cols, h * LANES:(h + 1) * LANES] = (
                v_pages[r][pl.ds(h, PAGE_SIZE, stride=DA_KV_HEADS), :].astype(BF16))
    _softmax_update(_dot(qbd, kbuf[...]), vbuf[...], m_sc, l_sc, acc_sc)

    @pl.when(c == n_chunks - 1)
    def _():
        n_rows = qbd.shape[0]
        row = lax.broadcasted_iota(jnp.int32, (n_rows, LANES), 0)
        col = lax.broadcasted_iota(jnp.int32, (n_rows, LANES), 1)
        t_row = (row >> 1) & (N_NEW - 1)
        s = jnp.where(col <= t_row, _dot(qbd, kn_ref[...]), NEG)
        _softmax_update(s, vn_ref[...], m_sc, l_sc, acc_sc)

        lam = _lambda_value(lam_ref[...], lam_init)
        inv = 1.0 / l_sc[...]
        acc = acc_sc[...]
        half = N_NEW * DA_GROUP
        for h in range(DA_KV_HEADS):
            rows = slice(h * 2 * half, (h + 1) * 2 * half)
            blk = acc[rows, h * LANES:(h + 1) * LANES] * inv[rows, :]
            o = blk[:half] - lam * blk[half:]
            o_ref[h] = _rms(o, gsub_ref[...]) * (1.0 - lam_init)


def _da_sample_kernel(pt_ref, lam_ref, qbd_ref, kn_ref, vn_ref, gsub_ref, *rest, n_pp, lam_init):
    del pt_ref
    _da_sample_chunk(pl.program_id(1), pl.num_programs(1), lam_ref, qbd_ref, kn_ref, vn_ref, gsub_ref,
                     rest[:n_pp], rest[n_pp:2 * n_pp], rest[2 * n_pp], *rest[2 * n_pp + 1:], lam_init)


def _da_sample_operands(q_s, kb_s, vb_s, cache_k, cache_v):
    n_seq = q_s.shape[0] // N_NEW
    kvw = DA_KV_HEADS * 2 * DA_HEAD
    n_rows = DA_KV_HEADS * 2 * N_NEW * DA_GROUP
    q6 = q_s.reshape(n_seq, N_NEW, DA_KV_HEADS, DA_GROUP, 2, DA_HEAD)
    eye_h = jnp.eye(DA_KV_HEADS, dtype=BF16)
    eye_m = jnp.eye(2, dtype=BF16)
    qbd = jnp.einsum('btkgmd,kK,mM->bKMtgkmd', q6, eye_h, eye_m).reshape(n_seq, n_rows, kvw)
    k_new = jnp.pad(kb_s.reshape(n_seq, N_NEW, kvw).transpose(0, 2, 1), ((0, 0), (0, 0), (0, LANES - N_NEW)))
    v_new = jnp.pad(vb_s.reshape(n_seq, N_NEW, kvw), ((0, 0), (0, LANES - N_NEW), (0, 0)))
    n_pool = cache_k.shape[1]
    ck = jnp.transpose(cache_k, (0, 1, 3, 4, 5, 2)).reshape(cache_k.shape[0], n_pool, kvw, PAGE_SIZE)
    cv = cache_v.reshape(cache_v.shape[0], n_pool, PAGE_SIZE * DA_KV_HEADS, 2 * DA_HEAD)
    return qbd, k_new, v_new, ck, cv


def _da_sample_output(out):
    n_seq = out.shape[0]
    out = out.reshape(n_seq, DA_KV_HEADS, N_NEW, DA_GROUP, LANES).transpose(0, 2, 1, 3, 4)
    return out.reshape(n_seq * N_NEW, DA_HEADS * 2 * DA_HEAD).astype(BF16)


def _da_attend_sample(q_s, kb_s, vb_s, cache_k, cache_v, page_table, lam_p, g_sub, lam_init, layer):
    n_seq = q_s.shape[0] // N_NEW
    n_pages = page_table.shape[1]
    n_pp = math.gcd(PAGES_PER_STEP, n_pages)
    n_chunks = n_pages // n_pp
    kvw = DA_KV_HEADS * 2 * DA_HEAD
    n_rows = DA_KV_HEADS * 2 * N_NEW * DA_GROUP
    qbd, k_new, v_new, ck, cv = _da_sample_operands(q_s, kb_s, vb_s, cache_k, cache_v)

    def page_spec(r):
        return pl.BlockSpec((None, None, kvw, PAGE_SIZE),
                            lambda b, c, pt: (layer, pt[b * n_pages + c * n_pp + r], 0, 0))

    per_seq = lambda shape: pl.BlockSpec((None,) + shape, lambda b, c, pt: (b,) + (0,) * len(shape))
    const = lambda shape: pl.BlockSpec(shape, lambda b, c, pt: (0,) * len(shape))
    kern = functools.partial(_da_sample_kernel, n_pp=n_pp, lam_init=lam_init)
    out = pl.pallas_call(
        kern,
        grid_spec=pltpu.PrefetchScalarGridSpec(
            num_scalar_prefetch=1,
            grid=(n_seq, n_chunks),
            in_specs=[const(lam_p.shape), per_seq((n_rows, kvw)), per_seq((kvw, LANES)),
                      per_seq((LANES, kvw)), const((1, LANES))]
                     + [page_spec(r) for r in range(n_pp)] + [page_spec(r) for r in range(n_pp)],
            out_specs=per_seq((DA_KV_HEADS, N_NEW * DA_GROUP, LANES)),
            scratch_shapes=[pltpu.VMEM((kvw, n_pp * PAGE_SIZE), BF16),
                            pltpu.VMEM((n_pp * PAGE_SIZE, kvw), BF16),
                            pltpu.VMEM((n_rows, LANES), F32), pltpu.VMEM((n_rows, LANES), F32),
                            pltpu.VMEM((n_rows, kvw), F32)]),
        out_shape=jax.ShapeDtypeStruct((n_seq, DA_KV_HEADS, N_NEW * DA_GROUP, LANES), F32),
        compiler_params=_params(("parallel", "arbitrary")),
        name="da_attn_sample",
    )(page_table.reshape(-1), lam_p, qbd, k_new, v_new, g_sub.reshape(1, LANES).astype(F32),
      *([ck] * n_pp), *([cv] * n_pp))
    return _da_sample_output(out)


def _post_kernel(x_ref, a_ref, wo_ref, gffn_ref, wgu_ref, wd_ref, o_ref, *, n_chunks):
    x1 = x_ref[...] + _dot(a_ref[...], wo_ref[...])
    h = _rms(x1, gffn_ref[...]).astype(BF16)
    ck = D_FF // n_chunks
    acc = x1
    for c in range(n_chunks):
        g = _dot(h, wgu_ref[:, c * ck:(c + 1) * ck])
        u = _dot(h, wgu_ref[:, D_FF + c * ck:D_FF + (c + 1) * ck])
        act = (g * jax.nn.sigmoid(g) * u).astype(BF16)
        acc = acc + _dot(act, wd_ref[c * ck:(c + 1) * ck, :])
    o_ref[...] = acc


def _post(x, a, consts):
    t = x.shape[0]
    tm = _pick_tile(t)
    ka = a.shape[1]
    kern = functools.partial(_post_kernel, n_chunks=2)
    return pl.pallas_call(
        kern,
        grid=(t // tm,),
        in_specs=[pl.BlockSpec((tm, D_MODEL), lambda i: (i, 0)),
                  pl.BlockSpec((tm, ka), lambda i: (i, 0))] + [_const_spec(c.shape) for c in consts],
        out_specs=pl.BlockSpec((tm, D_MODEL), lambda i: (i, 0)),
        out_shape=jax.ShapeDtypeStruct((t, D_MODEL), F32),
        compiler_params=_params(("parallel",)),
        name="post_ffn",
    )(x, a, *consts)


MLA_BLK = 128
ROPE_HALF = MLA_ROPE // 2


def _mla_proj_kernel(x_ref, cos_ref, sin_ref, gmix_ref, wd_ref, gqa_ref, gkva_ref, gkra_ref, gkrb_ref,
                     wuqa_ref, wuqb_ref, ga_ref, gb_ref, invn_ref, onesq_ref,
                     wuk_ref, gkn_ref, onesk_ref, wuv_ref,
                     qcat_ref, kcat_ref, v_ref, ckvf_ref, kr_ref):
    h = _rms(x_ref[...], gmix_ref[...]).astype(BF16)
    d = _dot(h, wd_ref[...])
    cq = _rms(d[:, :MLA_Q_LORA], gqa_ref[...]).astype(BF16)
    ckv = _rms(d[:, MLA_Q_LORA:MLA_Q_LORA + MLA_KV_LORA], gkva_ref[...])
    ckvf_ref[...] = ckv
    ckv_b = ckv.astype(BF16)
    cos = cos_ref[...]
    sin = sin_ref[...]
    base = MLA_Q_LORA + MLA_KV_LORA
    ka = d[:, base:base + MLA_BLK]
    kb = d[:, base + MLA_BLK:base + 2 * MLA_BLK]
    rk = lax.rsqrt(jnp.sum(ka * ka, axis=-1, keepdims=True) * (1.0 / MLA_ROPE) + EPS)
    kr = rk * (ka * gkra_ref[...] * cos + kb * gkrb_ref[...] * sin)
    kr_ref[...] = kr
    ya = _dot(cq, wuqa_ref[...])
    yb = _dot(cq, wuqb_ref[...])
    reps = MXU_DIM // MLA_BLK
    cos2 = jnp.tile(cos, (1, reps))
    sin2 = jnp.tile(sin, (1, reps))
    kr2 = jnp.tile(kr, (1, reps))
    onesq = onesq_ref[...]
    onesk = onesk_ref[...]
    kk = _dot(ckv_b, wuk_ref[...])
    for c in range(MLA_HEADS * MLA_BLK // MXU_DIM):
        sl = slice(c * MXU_DIM, (c + 1) * MXU_DIM)
        yac = ya[:, sl]
        ss = _dot((yac * yac).astype(BF16), onesq)
        r = lax.rsqrt(ss * invn_ref[:, sl] + EPS)
        qc = r * (yac * ga_ref[:, sl] * cos2 + yb[:, sl] * gb_ref[:, sl] * sin2)
        qcat_ref[:, sl] = qc.astype(BF16)
        kkc = kk[:, sl]
        ssk = _dot((kkc * kkc).astype(BF16), onesk)
        kn = kkc * lax.rsqrt(ssk * (1.0 / MLA_NOPE) + EPS) * gkn_ref[:, sl]
        kcat_ref[:, sl] = (kn + kr2).astype(BF16)
    v_ref[...] = _dot(ckv_b, wuv_ref[...]).astype(BF16)


def _head_block(rope_x1, rope_x2, nope):
    lead = (rope_x1 if rope_x1 is not None else nope).shape[:-1]
    z = lambda n: jnp.zeros(lead + (n,), F32)
    parts = [rope_x1 if rope_x1 is not None else z(ROPE_HALF),
             rope_x2 if rope_x2 is not None else z(ROPE_HALF),
             nope if nope is not None else z(MLA_NOPE),
             z(MLA_BLK - MLA_ROPE - MLA_NOPE)]
    return jnp.concatenate(parts, axis=-1)


def _rope_tables(pos):
    inv = ROPE_THETA ** (-jnp.arange(0, MLA_ROPE, 2, dtype=F32) / MLA_ROPE)
    ang = pos.astype(F32)[:, None] * inv[None, :]
    cs, sn = jnp.cos(ang), jnp.sin(ang)
    return _head_block(cs, cs, jnp.ones((pos.shape[0], MLA_NOPE), F32)), _head_block(-sn, sn, None)


def _mla_proj_consts(gmix, w_down, g_qa, g_kva, w_uq, w_uk, w_uv, g_qn, g_qr, g_kn, g_kr):
    hb = MLA_HEADS * MLA_BLK
    wd = w_down.astype(F32)
    base = MLA_Q_LORA + MLA_KV_LORA
    kx1, kx2 = wd[:, base:base + ROPE_HALF], wd[:, base + ROPE_HALF:base + MLA_ROPE]
    wd_ext = jnp.concatenate([wd[:, :base], _head_block(kx1, kx2, None), _head_block(kx2, kx1, None)],
                             axis=1).astype(BF16)
    g1, g2 = g_kr[:ROPE_HALF].astype(F32), g_kr[ROPE_HALF:].astype(F32)
    gkr_a = _head_block(g1, g2, None).reshape(1, MLA_BLK)
    gkr_b = _head_block(g2, g1, None).reshape(1, MLA_BLK)
    wq = w_uq.astype(F32).reshape(MLA_Q_LORA, MLA_HEADS, MLA_NOPE + MLA_ROPE)
    qn_w, q1_w, q2_w = wq[..., :MLA_NOPE], wq[..., MLA_NOPE:MLA_NOPE + ROPE_HALF], wq[..., MLA_NOPE + ROPE_HALF:]
    wuq_a = _head_block(q1_w, q2_w, qn_w).reshape(MLA_Q_LORA, hb).astype(BF16)
    wuq_b = _head_block(q2_w, q1_w, None).reshape(MLA_Q_LORA, hb).astype(BF16)
    r1, r2 = g_qr[:ROPE_HALF].astype(F32), g_qr[ROPE_HALF:].astype(F32)
    ga = jnp.tile(_head_block(r1, r2, g_qn.astype(F32)), MLA_HEADS).reshape(1, hb) * (MLA_SCALE * LOG2E)
    gb = jnp.tile(_head_block(r2, r1, None), MLA_HEADS).reshape(1, hb) * (MLA_SCALE * LOG2E)
    invn = jnp.tile(_head_block(jnp.full((ROPE_HALF,), 1.0 / MLA_ROPE, F32),
                                jnp.full((ROPE_HALF,), 1.0 / MLA_ROPE, F32),
                                jnp.full((MLA_NOPE,), 1.0 / MLA_NOPE, F32)), MLA_HEADS).reshape(1, hb)
    blk_groups = lambda groups: [(a + o, b + o) for o in range(0, MXU_DIM, MLA_BLK) for a, b in groups]
    ones_q = _group_ones(MXU_DIM, blk_groups([(0, MLA_ROPE), (MLA_ROPE, MLA_ROPE + MLA_NOPE)]))
    ones_k = _group_ones(MXU_DIM, blk_groups([(MLA_ROPE, MLA_ROPE + MLA_NOPE)]))
    wuk = _head_block(None, None, w_uk.astype(F32)).reshape(MLA_KV_LORA, hb).astype(BF16)
    gkn = jnp.tile(_head_block(None, None, g_kn.astype(F32)), MLA_HEADS).reshape(1, hb)
    wuv = w_uv.reshape(MLA_KV_LORA, MLA_HEADS * MLA_V).astype(BF16)
    return (gmix.reshape(1, D_MODEL).astype(F32), wd_ext,
            g_qa.reshape(1, -1).astype(F32), g_kva.reshape(1, -1).astype(F32), gkr_a, gkr_b,
            wuq_a, wuq_b, ga, gb, invn, ones_q, wuk, gkn, ones_k, wuv)


def _mla_project(x, pos, consts):
    t = x.shape[0]
    tm = _pick_tile(t)
    hb = MLA_HEADS * MLA_BLK
    cos_t, sin_t = _rope_tables(pos)
    row = lambda n: pl.BlockSpec((tm, n), lambda i: (i, 0))
    return pl.pallas_call(
        _mla_proj_kernel,
        grid=(t // tm,),
        in_specs=[row(D_MODEL), row(MLA_BLK), row(MLA_BLK)] + [_const_spec(c.shape) for c in consts],
        out_specs=[row(hb), row(hb), row(MLA_HEADS * MLA_V), row(MLA_KV_LORA), row(MLA_BLK)],
        out_shape=[jax.ShapeDtypeStruct((t, hb), BF16), jax.ShapeDtypeStruct((t, hb), BF16),
                   jax.ShapeDtypeStruct((t, MLA_HEADS * MLA_V), BF16),
                   jax.ShapeDtypeStruct((t, MLA_KV_LORA), F32),
                   jax.ShapeDtypeStruct((t, MLA_BLK), F32)],
        compiler_params=_params(("parallel",)),
        name="mla_proj",
    )(x, cos_t, sin_t, *consts)


def _mla_attn_kernel(q_ref, k_ref, v_ref, o_ref, m_sc, l_sc, acc_sc, *, tq):
    i = pl.program_id(2)
    rows = 2 * tq
    _softmax_init(m_sc, l_sc, acc_sc)

    def step(j, masked):
        off = pl.multiple_of(j * tq, tq)
        s = jnp.concatenate(
            [_dot_nt(q_ref[:, a * LANES:(a + 1) * LANES], k_ref[pl.ds(off, tq), a * LANES:(a + 1) * LANES])
             for a in range(2)], axis=0)
        if masked:
            r = lax.broadcasted_iota(jnp.int32, (rows, tq), 0) & (tq - 1)
            c = lax.broadcasted_iota(jnp.int32, (rows, tq), 1)
            s = jnp.where(c <= r, s, NEG)
        _softmax_update(s, v_ref[pl.ds(off, tq), :], m_sc, l_sc, acc_sc)

    def body(jj, carry):
        step(2 * jj, False)
        step(2 * jj + 1, False)
        return carry

    lax.fori_loop(0, i // 2, body, 0)

    @pl.when(i % 2 == 1)
    def _():
        step(i - 1, False)

    step(i, True)
    a = acc_sc[...] / l_sc[...]
    lane = lax.broadcasted_iota(jnp.int32, (tq, LANES), 1)
    o_ref[...] = jnp.where(lane < MLA_V, a[:tq], a[tq:]).astype(BF16)


def _mla_attend_prompt(qcat, kcat, v, batch, seq):
    tq = _pick_tile(seq, (512, 256, 128))
    nq = seq // tq
    kern = functools.partial(_mla_attn_kernel, tq=tq)
    return pl.pallas_call(
        kern,
        grid=(batch, MLA_HEADS // 2, nq),
        in_specs=[pl.BlockSpec((tq, 2 * MLA_BLK), lambda b, h, i: (b * nq + i, h)),
                  pl.BlockSpec((seq, 2 * MLA_BLK), lambda b, h, i: (b, h)),
                  pl.BlockSpec((seq, 2 * MLA_V), lambda b, h, i: (b, h))],
        out_specs=pl.BlockSpec((tq, 2 * MLA_V), lambda b, h, i: (b * nq + i, h)),
        out_shape=jax.ShapeDtypeStruct((batch * seq, MLA_HEADS * MLA_V), BF16),
        scratch_shapes=[pltpu.VMEM((2 * tq, LANES), F32)] * 3,
        compiler_params=_params(("parallel", "parallel", "arbitrary")),
        name="mla_attn_prompt",
    )(qcat, kcat, v)


def _mla_sample_kernel(pt_ref, qn_ref, qr_ref, cn_ref, rn_ref, wuk_ref, wukt_ref, gkn_ref, wuv_ref, *rest, n_pp):
    c_pages = rest[:n_pp]
    r_pages = rest[n_pp:2 * n_pp]
    o_ref = rest[2 * n_pp]
    lhs_sc, cbuf, rbuf, m_sc, l_sc, acc_sc = rest[2 * n_pp + 1:]
    del pt_ref
    c = pl.program_id(1)
    nw = MLA_HEADS * MLA_NOPE
    n_q = N_NEW * MLA_HEADS

    @pl.when(c == 0)
    def _():
        _softmax_init(m_sc, l_sc, acc_sc)
        qn = (qn_ref[...].astype(F32) * gkn_ref[...]).astype(BF16)
        lhs_sc[:nw, :] = wukt_ref[...]
        lhs_sc[nw:, :] = _dot_nt(qn, wuk_ref[...]).astype(BF16)

    head8 = lax.broadcasted_iota(jnp.int32, (MLA_HEADS, LANES), 0)

    def scores(cb, rt):
        out = _dot_nt(lhs_sc[...], cb)
        n = out.shape[1]
        ssq = jnp.zeros((MLA_HEADS, n), F32)
        for h in range(MLA_HEADS):
            kk = out[h * MLA_NOPE:(h + 1) * MLA_NOPE]
            ss = jnp.sum(kk * kk, axis=0, keepdims=True)
            ssq = jnp.where(jnp.tile(head8, (1, n // LANES)) == h, ss, ssq)
        rinv = lax.rsqrt(ssq * (1.0 / MLA_NOPE) + EPS)
        return out[nw:] * jnp.tile(rinv, (N_NEW, 1)) + _dot(qr_ref[...], rt)

    for r in range(n_pp):
        cbuf[r * PAGE_SIZE:(r + 1) * PAGE_SIZE, :] = c_pages[r][...].astype(BF16)
        rbuf[:, r * PAGE_SIZE:(r + 1) * PAGE_SIZE] = r_pages[r][...].astype(BF16)
    _softmax_update(scores(cbuf[...], rbuf[...]), cbuf[...], m_sc, l_sc, acc_sc)

    @pl.when(c == pl.num_programs(1) - 1)
    def _():
        row = lax.broadcasted_iota(jnp.int32, (n_q, LANES), 0)
        col = lax.broadcasted_iota(jnp.int32, (n_q, LANES), 1)
        cn = cn_ref[...].astype(BF16)
        s = jnp.where(col <= row // MLA_HEADS, scores(cn, rn_ref[...].astype(BF16)), NEG)
        _softmax_update(s, cn, m_sc, l_sc, acc_sc)
        inv = jnp.tile(1.0 / l_sc[...], (1, MLA_KV_LORA // LANES))
        o_lat = (acc_sc[...] * inv).astype(BF16)
        o2 = _dot(o_lat, wuv_ref[...])
        hrow = lax.broadcasted_iota(jnp.int32, (MLA_HEADS, MLA_HEADS * MLA_V), 0)
        hcol = lax.broadcasted_iota(jnp.int32, (MLA_HEADS, MLA_HEADS * MLA_V), 1) // MLA_V
        keep = hrow == hcol
        outs = [jnp.sum(jnp.where(keep, o2[t * MLA_HEADS:(t + 1) * MLA_HEADS, :], 0.0), axis=0, keepdims=True)
                for t in range(N_NEW)]
        o_ref[...] = jnp.concatenate(outs, axis=0)


def _mla_attend_sample(qcat_s, ckv_s, kr_s, cache_ckv, cache_kr, page_table, w_uk, g_kn, w_uv, layer):
    n_seq = qcat_s.shape[0] // N_NEW
    n_pages = page_table.shape[1]
    n_pp = math.gcd(MLA_PAGES_PER_STEP, n_pages)
    n_chunks = n_pages // n_pp
    nw = MLA_HEADS * MLA_NOPE
    n_q = N_NEW * MLA_HEADS
    q4 = qcat_s.reshape(n_seq, N_NEW, MLA_HEADS, MLA_BLK)
    q_rope = q4[..., :MLA_ROPE].reshape(n_seq, n_q, MLA_ROPE)
    q_nope = q4[..., MLA_ROPE:MLA_ROPE + MLA_NOPE]
    eye_h = jnp.eye(MLA_HEADS, dtype=BF16)
    qn_bd = jnp.einsum('bthd,hH->bthHd', q_nope, eye_h).reshape(n_seq, n_q, nw)
    c_new = jnp.pad(ckv_s.reshape(n_seq, N_NEW, MLA_KV_LORA), ((0, 0), (0, LANES - N_NEW), (0, 0)))
    r_new = jnp.pad(kr_s.reshape(n_seq, N_NEW, MLA_ROPE).transpose(0, 2, 1), ((0, 0), (0, 0), (0, LANES - N_NEW)))
    wuk = w_uk.reshape(MLA_KV_LORA, nw).astype(BF16)
    cache_kr_t = jnp.swapaxes(cache_kr, 2, 3)

    def page_spec(r, shape):
        return pl.BlockSpec((None, None) + shape,
                            lambda b, c, pt: (layer, pt[b * n_pages + c * n_pp + r], 0, 0))

    per_seq = lambda shape: pl.BlockSpec((None,) + shape, lambda b, c, pt: (b,) + (0,) * len(shape))
    const = lambda shape: pl.BlockSpec(shape, lambda b, c, pt: (0,) * len(shape))
    kern = functools.partial(_mla_sample_kernel, n_pp=n_pp)
    out = pl.pallas_call(
        kern,
        grid_spec=pltpu.PrefetchScalarGridSpec(
            num_scalar_prefetch=1,
            grid=(n_seq, n_chunks),
            in_specs=[per_seq((n_q, nw)), per_seq((n_q, MLA_ROPE)), per_seq((LANES, MLA_KV_LORA)),
                      per_seq((MLA_ROPE, LANES)), const((MLA_KV_LORA, nw)), const((nw, MLA_KV_LORA)),
                      const((1, nw)), const((MLA_KV_LORA, MLA_HEADS * MLA_V))]
                     + [page_spec(r, (PAGE_SIZE, MLA_KV_LORA)) for r in range(n_pp)]
                     + [page_spec(r, (MLA_ROPE, PAGE_SIZE)) for r in range(n_pp)],
            out_specs=per_seq((N_NEW, MLA_HEADS * MLA_V)),
            scratch_shapes=[pltpu.VMEM((nw + n_q, MLA_KV_LORA), BF16),
                            pltpu.VMEM((n_pp * PAGE_SIZE, MLA_KV_LORA), BF16),
                            pltpu.VMEM((MLA_ROPE, n_pp * PAGE_SIZE), BF16),
                            pltpu.VMEM((n_q, LANES), F32), pltpu.VMEM((n_q, LANES), F32),
                            pltpu.VMEM((n_q, MLA_KV_LORA), F32)]),
        out_shape=jax.ShapeDtypeStruct((n_seq, N_NEW, MLA_HEADS * MLA_V), F32),
        compiler_params=_params(("parallel", "arbitrary")),
        name="mla_attn_sample",
    )(page_table.reshape(-1), qn_bd, q_rope, c_new, r_new, wuk, wuk.T,
      jnp.tile(g_kn.astype(F32), MLA_HEADS).reshape(1, nw),
      w_uv.reshape(MLA_KV_LORA, MLA_HEADS * MLA_V).astype(BF16),
      *([cache_ckv] * n_pp), *([cache_kr_t] * n_pp))
    return out.reshape(n_seq * N_NEW, MLA_HEADS * MLA_V).astype(BF16)


def _hg_proj_kernel(x_ref, gmix_ref, w_ref, lbl_ref, q_ref, k_ref, lf_ref, v_ref, g_ref, *, layer):
    h = _rms(x_ref[...], gmix_ref[...]).astype(BF16)
    z = _dot(h, w_ref[...])
    lg = lbl_ref[...]
    e = jnp.exp(lg - jnp.max(lg, axis=0, keepdims=True))
    sm = e / jnp.sum(e, axis=0, keepdims=True)
    lb = jnp.sum(sm[:layer + 1], axis=0, keepdims=True) - sm[0:1]
    nk = HG_HEADS * HG_DK
    zq = z[:, :nk]
    q_ref[...] = zq * jax.nn.sigmoid(zq)
    f = lb + (1.0 - lb) * jax.nn.sigmoid(z[:, nk:2 * nk])
    k_ref[...] = 1.0 - f
    lf_ref[...] = jnp.log(f)
    v_ref[...] = z[:, 2 * nk:3 * nk]
    g_ref[...] = z[:, 3 * nk:]


def _hg_project(x, consts, layer):
    t = x.shape[0]
    tm = _pick_tile(t)
    nk = HG_HEADS * HG_DK
    row = lambda n: pl.BlockSpec((tm, n), lambda i: (i, 0))
    kern = functools.partial(_hg_proj_kernel, layer=layer)
    return pl.pallas_call(
        kern,
        grid=(t // tm,),
        in_specs=[row(D_MODEL)] + [_const_spec(c.shape) for c in consts],
        out_specs=[row(nk)] * 5,
        out_shape=[jax.ShapeDtypeStruct((t, nk), F32)] * 5,
        compiler_params=_params(("parallel",)),
        name="hg_proj",
    )(x, *consts)


def _hg_out(o, g, gout):
    return (_rms(o, gout) * (g * jax.nn.sigmoid(g))).astype(BF16)


def _hg_level_tables(chunk):
    r = jnp.arange(chunk)[:, None]
    u = jnp.arange(chunk)[None, :]
    mats, masks = [], []
    z = chunk // 2
    while z >= 1:
        ref = (r // (2 * z)) * (2 * z) + z - 1
        mats.append(jnp.where(r > ref, (u > ref) & (u <= r), (u > r) & (u <= ref)))
        masks.append((r // (2 * z) == u // (2 * z)) & (r % (2 * z) >= z) & (u % (2 * z) < z))
        z //= 2
    mats.append(u <= r)
    masks.append(u == r)
    mstack = jnp.concatenate(mats, axis=0).astype(BF16)
    return jnp.concatenate([mstack, mstack], axis=1), jnp.stack(masks).astype(F32)


def _hg_chunk_kernel(q_ref, k_ref, lf_ref, v_ref, g_ref, gout_ref, mstack_ref, mask_ref,
                     o_ref, sfin_ref, s_sc, *, chunk):
    n = pl.program_id(1)

    @pl.when(n == 0)
    def _():
        s_sc[...] = jnp.zeros(s_sc.shape, F32)

    n_lev = mask_ref.shape[0] - 1
    lf = lf_ref[...]
    hi = lf.astype(BF16)
    lo = (lf - hi.astype(F32)).astype(BF16)
    dec = _dot(mstack_ref[...], jnp.concatenate([hi, lo], axis=0))

    for h in range(HG_HEADS):
        sl = slice(h * HG_DK, (h + 1) * HG_DK)
        q = q_ref[:, sl]
        k = k_ref[:, sl]
        vb = v_ref[:, sl].astype(BF16)
        a = _dot_nt(q.astype(BF16), k.astype(BF16)) * mask_ref[n_lev]
        for lev in range(n_lev):
            gfac = jnp.exp(dec[lev * chunk:(lev + 1) * chunk, sl])
            a = a + _dot_nt((q * gfac).astype(BF16), (k * gfac).astype(BF16)) * mask_ref[lev]
        cu = dec[n_lev * chunk:, sl]
        state = s_sc[h]
        o = _dot(a.astype(BF16), vb) + _dot((q * jnp.exp(cu)).astype(BF16), state.astype(BF16))
        o_ref[:, sl] = _hg_out(o, g_ref[:, sl], gout_ref[...])
        last = cu[chunk - 1:chunk, :]
        k_dec = (k * jnp.exp(last - cu)).T.astype(BF16)
        s_sc[h] = state * _col_replicated(jnp.exp(last)) + _dot(k_dec, vb)

    @pl.when(n == pl.num_programs(1) - 1)
    def _():
        sfin_ref[...] = s_sc[...]


def _hg_recur_prompt(q, k, lf, v, g, g_out, batch, seq):
    chunk = HG_CHUNK
    nc = seq // chunk
    nk = HG_HEADS * HG_DK
    mstack, masks = _hg_level_tables(chunk)
    blk = pl.BlockSpec((chunk, nk), lambda b, n: (b * nc + n, 0))
    kern = functools.partial(_hg_chunk_kernel, chunk=chunk)
    return pl.pallas_call(
        kern,
        grid=(batch, nc),
        in_specs=[blk] * 5 + [_const_spec((1, HG_DV)), _const_spec(mstack.shape), _const_spec(masks.shape)],
        out_specs=[blk, pl.BlockSpec((None, HG_HEADS, HG_DK, HG_DV), lambda b, n: (b, 0, 0, 0))],
        out_shape=[jax.ShapeDtypeStruct((batch * seq, nk), BF16),
                   jax.ShapeDtypeStruct((batch, HG_HEADS, HG_DK, HG_DV), F32)],
        scratch_shapes=[pltpu.VMEM((HG_HEADS, HG_DK, HG_DV), F32)],
        compiler_params=_params(("parallel", "arbitrary")),
        name="hg_recur_prompt",
    )(q, k, lf, v, g, g_out.reshape(1, HG_DV).astype(F32), mstack, masks)


def _hg_sample_kernel(q_ref, k_ref, lf_ref, v_ref, g_ref, gout_ref, s0_ref, o_ref, s_ref, *, n_blk):
    row8 = lax.broadcasted_iota(jnp.int32, (8, LANES), 0)

    def rows_to_tile(rows):
        top = jnp.zeros((8, LANES), F32)
        for t, r in enumerate(rows):
            top = jnp.where(row8 == t, r, top)
        return jnp.concatenate([top, jnp.zeros((LANES - 8, LANES), F32)], axis=0)

    def per_seq(b, carry):
        for h in range(HG_HEADS):
            sl = slice(h * HG_DK, (h + 1) * HG_DK)
            q = [q_ref[b, t:t + 1, sl] for t in range(N_NEW)]
            k = [k_ref[b, t:t + 1, sl] for t in range(N_NEW)]
            v = [v_ref[b, t:t + 1, sl] for t in range(N_NEW)]
            cum = []
            for t in range(N_NEW):
                lf_t = lf_ref[b, t:t + 1, sl]
                cum.append(lf_t if t == 0 else cum[-1] + lf_t)
            s0 = s0_ref[b, h]
            q_dec = rows_to_tile([q[t] * jnp.exp(cum[t]) for t in range(N_NEW)])[:16]
            inter = _dot(q_dec.astype(BF16), s0.astype(BF16))
            for t in range(N_NEW):
                o = inter[t:t + 1]
                for s in range(t + 1):
                    a = jnp.sum(q[t] * k[s] * jnp.exp(cum[t] - cum[s]), axis=-1, keepdims=True)
                    o = o + a * v[s]
                o_ref[b, t:t + 1, sl] = _rms(o, gout_ref[...]) * (
                    g_ref[b, t:t + 1, sl] * jax.nn.sigmoid(g_ref[b, t:t + 1, sl]))
            last = cum[-1]
            k_dec = rows_to_tile([k[s] * jnp.exp(last - cum[s]) for s in range(N_NEW)])
            v_tile = rows_to_tile(v)
            s_ref[b, h] = (s0 * _col_replicated(jnp.exp(last))
                           + _dot(k_dec.T.astype(BF16), v_tile.astype(BF16)))
        return carry

    lax.fori_loop(0, n_blk, per_seq, 0)


def _hg_recur_sample(q, k, lf, v, g, g_out, state):
    n_seq, _, nk = q.shape
    n_blk = _pick_tile(n_seq, (HG_SEQ_BLOCK, 4, 2, 1))
    tok = pl.BlockSpec((n_blk, N_NEW, nk), lambda i: (i, 0, 0))
    st = pl.BlockSpec((n_blk, HG_HEADS, HG_DK, HG_DV), lambda i: (i, 0, 0, 0))
    kern = functools.partial(_hg_sample_kernel, n_blk=n_blk)
    return pl.pallas_call(
        kern,
        grid=(n_seq // n_blk,),
        in_specs=[tok] * 5 + [pl.BlockSpec((1, HG_DV), lambda i: (0, 0)), st],
        out_specs=[tok, st],
        out_shape=[jax.ShapeDtypeStruct((n_seq, N_NEW, nk), F32),
                   jax.ShapeDtypeStruct(state.shape, F32)],
        compiler_params=_params(("parallel",)),
        name="hg_recur_sample",
    )(q, k, lf, v, g, g_out.reshape(1, HG_DV).astype(F32), state.astype(F32))


def kernel(x_prompt, x_sample, cache_da_k, cache_da_v, cache_mla_ckv, cache_mla_krope, state_hgrn, page_table, norm_mix, norm_ffn, ffn_w_gu, ffn_w_down, da_w_qkv, da_q_gain, da_k_gain, da_lambda, da_sub_gain, da_w_o, mla_w_down, mla_qa_gain, mla_kva_gain, mla_w_uq, mla_w_uk, mla_w_uv, mla_qn_gain, mla_qr_gain, mla_kn_gain, mla_kr_gain, mla_w_o, hg_w_in, hg_lb_logits, hg_out_gain, hg_w_o):
    batch, seq, _ = x_prompt.shape
    n_seq, n_new, _ = x_sample.shape
    assert n_new == N_NEW and seq % HG_CHUNK == 0
    tp, ts = batch * seq, n_seq * n_new
    past = page_table.shape[1] * PAGE_SIZE
    depth = norm_mix.shape[0]
    xp = x_prompt.reshape(tp, D_MODEL).astype(F32)
    xs = x_sample.reshape(ts, D_MODEL).astype(F32)
    pos_p = jnp.tile(jnp.arange(seq), batch)
    pos_s = jnp.tile(past + jnp.arange(n_new), n_seq)

    da_kp, da_vp, da_ks, da_vs = [], [], [], []
    mla_cp, mla_rp, mla_cs, mla_rs = [], [], [], []
    hg_p, hg_s = [], []
    for i in range(depth):
        kind, j = i % N_MIXERS, i // N_MIXERS
        if kind == 0:
            lam_init = 0.8 - 0.6 * math.exp(-0.3 * i)
            lam_p = da_lambda[j].astype(F32)
            consts = _da_proj_consts(norm_mix[i], da_w_qkv[j], da_q_gain[j], da_k_gain[j])
            q_p, kf_p, vf_p, kb_p, vb_p = _da_project(xp, consts, seq)
            q_s, kf_s, vf_s, kb_s, vb_s = _da_project(xs, consts)
            a_p = _da_attend_prompt(q_p, kb_p, vb_p, lam_p, da_sub_gain[j], lam_init, batch, seq)
            a_s = _da_attend_sample(q_s, kb_s, vb_s, cache_da_k, cache_da_v, page_table,
                                    lam_p, da_sub_gain[j], lam_init, j)
            w_o = da_w_o[j]
            da_kp.append(kf_p.reshape(batch, DA_KV_HEADS, 2, DA_HEAD, seq).transpose(0, 4, 1, 2, 3))
            da_vp.append(vf_p.reshape(batch, seq, DA_KV_HEADS, 2 * DA_HEAD))
            da_ks.append(kf_s.reshape(n_seq, n_new, DA_KV_HEADS, 2, DA_HEAD))
            da_vs.append(vf_s.reshape(n_seq, n_new, DA_KV_HEADS, 2 * DA_HEAD))
        elif kind == 1:
            consts = _mla_proj_consts(norm_mix[i], mla_w_down[j], mla_qa_gain[j], mla_kva_gain[j], mla_w_uq[j],
                                      mla_w_uk[j], mla_w_uv[j], mla_qn_gain[j], mla_qr_gain[j],
                                      mla_kn_gain[j], mla_kr_gain[j])
            qcat_p, kcat_p, v_p, ckv_p, kr_p = _mla_project(xp, pos_p, consts)
            qcat_s, _, _, ckv_s, kr_s = _mla_project(xs, pos_s, consts)
            kr_p, kr_s = kr_p[:, :MLA_ROPE], kr_s[:, :MLA_ROPE]
            a_p = _mla_attend_prompt(qcat_p, kcat_p, v_p, batch, seq)
            a_s = _mla_attend_sample(qcat_s, ckv_s, kr_s, cache_mla_ckv, cache_mla_krope,
                                     page_table, mla_w_uk[j], mla_kn_gain[j], mla_w_uv[j], j)
            w_o = mla_w_o[j]
            mla_cp.append(ckv_p.reshape(batch, seq, MLA_KV_LORA))
            mla_rp.append(kr_p.reshape(batch, seq, MLA_ROPE))
            mla_cs.append(ckv_s.reshape(n_seq, n_new, MLA_KV_LORA))
            mla_rs.append(kr_s.reshape(n_seq, n_new, MLA_ROPE))
        else:
            consts = (norm_mix[i].reshape(1, D_MODEL).astype(F32), hg_w_in[j].astype(BF16),
                      hg_lb_logits.astype(F32))
            hq, hk, hlf, hv, hgate = _hg_project(xp, consts, i)
            a_p, s_p = _hg_recur_prompt(hq, hk, hlf, hv, hgate, hg_out_gain[j], batch, seq)
            smp = lambda a: a.reshape(n_seq, n_new, HG_HEADS * HG_DK)
            a_s, s_s = _hg_recur_sample(*[smp(a) for a in _hg_project(xs, consts, i)],
                                        hg_out_gain[j], state_hgrn[j])
            a_s = a_s.reshape(ts, HG_HEADS * HG_DV).astype(BF16)
            w_o = hg_w_o[j]
            hg_p.append(s_p.astype(state_hgrn.dtype))
            hg_s.append(s_s.astype(state_hgrn.dtype))
        post_consts = (w_o.astype(BF16), norm_ffn[i].reshape(1, D_MODEL).astype(F32),
                       ffn_w_gu[i].astype(BF16), ffn_w_down[i].astype(BF16))
        xp = _post(xp, a_p, post_consts)
        xs = _post(xs, a_s, post_consts)

    return (xp.reshape(batch, seq, D_MODEL), xs.reshape(n_seq, n_new, D_MODEL),
            jnp.stack(da_kp), jnp.stack(da_vp), jnp.stack(da_ks), jnp.stack(da_vs),
            jnp.stack(mla_cp), jnp.stack(mla_rp), jnp.stack(mla_cs), jnp.stack(mla_rs),
            jnp.stack(hg_p), jnp.stack(hg_s))
```
